```python
import math
import jax, jax.numpy as jnp
from jax import lax
import numpy as np

D_MODEL = 4096
BATCH = 4
SEQ = 2048
DEPTH = 4
DEC_BATCH = 128
DEC_SEQ = 8
PAST_LEN = 16384
PAGE_SIZE = 128

N_MIXERS = 3
N_GLA = (DEPTH + 2) // 3
N_SC = (DEPTH + 1) // 3
N_CF = DEPTH // 3

GLA_HEADS = 4
GLA_DK = D_MODEL // 2
GLA_DV = D_MODEL
GLA_HK = GLA_DK // GLA_HEADS
GLA_HV = GLA_DV // GLA_HEADS
GLA_GATE_RANK = 16
GLA_TAU = 16.0
GLA_CHUNK = 16

SC_WIDTH = 3
CF_WIDTH = 31

MEM_LEN = 256
MEM_HEADS = 4
MEM_HD = 128
MEM_DIM = MEM_HEADS * MEM_HD

D_FF = 4 * D_MODEL

EPS = 1e-6

kernel_name = 'hybrid_gla_shortconv_conformer_memxattn_step'


def _rmsnorm(x, g):
    xf = x.astype(jnp.float32)
    y = xf * lax.rsqrt(jnp.mean(xf * xf, axis=-1, keepdims=True) + EPS) * g.astype(jnp.float32)
    return y.astype(x.dtype)


def _layernorm(x, g, b):
    xf = x.astype(jnp.float32)
    mu = jnp.mean(xf, axis=-1, keepdims=True)
    var = jnp.mean(jnp.square(xf - mu), axis=-1, keepdims=True)
    y = (xf - mu) * lax.rsqrt(var + EPS) * g.astype(jnp.float32) + b.astype(jnp.float32)
    return y.astype(x.dtype)


def _depthwise_causal_conv(ext, w):
    k_w, d = w.shape
    return lax.conv_general_dilated(ext, w[:, None, :].astype(ext.dtype), window_strides=(1,),
                                    padding='VALID', dimension_numbers=('NWC', 'WIO', 'NWC'),
                                    feature_group_count=d)


def _gla_scan(q, k, v, g, s0):
    bsz, seq = q.shape[0], q.shape[1]
    c = math.gcd(seq, GLA_CHUNK)
    n = seq // c

    def chunks(t):
        return jnp.moveaxis(t.reshape((bsz, n, c) + t.shape[2:]), 1, 0)

    causal = jnp.tril(jnp.ones((c, c), dtype=bool))

    def step(s, inp):
        qc, kc, vc, gc = inp
        b = jnp.cumsum(gc, axis=1)
        b_last = b[:, -1]
        q_dec = qc * jnp.exp(b)
        att = jnp.einsum('bihd,bjhd->bhij', q_dec, kc * jnp.exp(-b))
        att = jnp.where(causal, att, 0.0)
        o = jnp.einsum('bhij,bjhe->bihe', att, vc) + jnp.einsum('bihd,bhde->bihe', q_dec, s)
        s_new = jnp.exp(b_last)[..., None] * s + jnp.einsum(
            'bjhd,bjhe->bhde', kc * jnp.exp(b_last[:, None] - b), vc)
        return s_new, o

    s_fin, o = lax.scan(step, s0, (chunks(q), chunks(k), chunks(v), chunks(g)))
    o = jnp.moveaxis(o, 0, 1).reshape((bsz, seq) + v.shape[2:])
    return o, s_fin


def _gla_mixer(h, s0, wq, wk, wv, wg1, wg2, bg, wr, norm_g, wo):
    bsz, seq, _ = h.shape
    f32 = jnp.float32
    q = (h @ wq).astype(f32).reshape(bsz, seq, GLA_HEADS, GLA_HK) * (GLA_HK ** -0.5)
    k = (h @ wk).astype(f32).reshape(bsz, seq, GLA_HEADS, GLA_HK)
    v = (h @ wv).astype(f32).reshape(bsz, seq, GLA_HEADS, GLA_HV)
    g = jax.nn.log_sigmoid(((h @ wg1) @ wg2 + bg).astype(f32)) / GLA_TAU
    g = g.reshape(bsz, seq, GLA_HEADS, GLA_HK)
    o, s_fin = _gla_scan(q, k, v, g, s0.astype(f32))
    o = o * lax.rsqrt(jnp.mean(o * o, axis=-1, keepdims=True) + EPS) * norm_g.astype(f32).reshape(GLA_HEADS, GLA_HV)
    o = o.reshape(bsz, seq, GLA_DV).astype(h.dtype) * jax.nn.silu(h @ wr)
    return o @ wo, s_fin


def _short_conv_mixer(h, buf, w_in, w_conv, w_out):
    b_gate, c_gate, hv = jnp.split(h @ w_in, 3, axis=-1)
    u = c_gate * hv
    ext = jnp.concatenate([buf.astype(u.dtype), u], axis=1)
    y = _depthwise_causal_conv(ext, w_conv)
    return (b_gate * y) @ w_out, ext[:, -(SC_WIDTH - 1):]


def _conformer_conv_mixer(h, buf, w1, b1, w_dw, b_dw, ln_g, ln_b, w2, b2):
    a, gate = jnp.split(h @ w1 + b1, 2, axis=-1)
    u = a * jax.nn.sigmoid(gate)
    ext = jnp.concatenate([buf.astype(u.dtype), u], axis=1)
    z = _depthwise_causal_conv(ext, w_dw) + b_dw
    z = jax.nn.silu(_layernorm(z, ln_g, ln_b))
    return z @ w2 + b2, ext[:, -(CF_WIDTH - 1):]


def _mem_kv(mem, wk, wv):
    bsz, m, _ = mem.shape
    k = jnp.einsum('bmd,nde->nbme', mem, wk).reshape(DEPTH, bsz, m, MEM_HEADS, MEM_HD)
    v = jnp.einsum('bmd,nde->nbme', mem, wv).reshape(DEPTH, bsz, m, MEM_HEADS, MEM_HD)
    return k, v


def _mem_attn(h, k, v, wq, wo):
    bsz, seq, _ = h.shape
    q = (h @ wq).reshape(bsz, seq, MEM_HEADS, MEM_HD).astype(jnp.float32)
    s = jnp.einsum('blhd,bmhd->bhlm', q, k.astype(jnp.float32)) * (MEM_HD ** -0.5)
    p = jax.nn.softmax(s, axis=-1)
    o = jnp.einsum('bhlm,bmhd->blhd', p, v.astype(jnp.float32)).reshape(bsz, seq, MEM_DIM)
    return o.astype(h.dtype) @ wo


def _sqrelu_mlp(h, w_up, w_down):
    return jnp.square(jax.nn.relu(h @ w_up)) @ w_down


def _trunk(x, mem_k, mem_v, gla_s0, sc_buf0, cf_buf0, w):
    gla_s, sc_buf, cf_buf = [], [], []
    for i in range(DEPTH):
        kind, j = i % N_MIXERS, i // N_MIXERS
        h = _rmsnorm(x, w['norm_mix'][i])
        if kind == 0:
            o, s = _gla_mixer(h, gla_s0[j], w['gla_wq'][j], w['gla_wk'][j], w['gla_wv'][j],
                              w['gla_wg1'][j], w['gla_wg2'][j], w['gla_bg'][j], w['gla_wr'][j],
                              w['gla_norm'][j], w['gla_wo'][j])
            gla_s.append(s)
        elif kind == 1:
            o, s = _short_conv_mixer(h, sc_buf0[j], w['sc_win'][j], w['sc_wconv'][j], w['sc_wout'][j])
            sc_buf.append(s)
        else:
            o, s = _conformer_conv_mixer(h, cf_buf0[j], w['cf_w1'][j], w['cf_b1'][j], w['cf_wdw'][j],
                                         w['cf_bdw'][j], w['cf_ln_g'][j], w['cf_ln_b'][j],
                                         w['cf_w2'][j], w['cf_b2'][j])
            cf_buf.append(s)
        x = x + o
        x = x + _mem_attn(_rmsnorm(x, w['norm_mem'][i]), mem_k[i], mem_v[i], w['mem_wq'][i], w['mem_wo'][i])
        x = x + _sqrelu_mlp(_rmsnorm(x, w['norm_ffn'][i]), w['ffn_up'][i], w['ffn_down'][i])
    y = _rmsnorm(x, w['norm_final'])
    return y, jnp.stack(gla_s), jnp.stack(sc_buf), jnp.stack(cf_buf)


def setup_inputs(seed: int = 0) -> dict:
    key = jax.random.key(seed)
    ks = iter(jax.random.split(key, 64))
    D = D_MODEL

    def nrm(shape, scale):
        return jax.random.normal(next(ks), shape, jnp.float32) * scale

    def gain(shape):
        return 1.0 + nrm(shape, 0.02)

    return {
        'x_prompt': nrm((BATCH, SEQ, D), 1.0),
        'x_sample': nrm((DEC_BATCH, DEC_SEQ, D), 1.0),
        'state_gla': nrm((N_GLA, DEC_BATCH, GLA_HEADS, GLA_HK, GLA_HV), 1.0),
        'state_sconv': nrm((N_SC, DEC_BATCH, SC_WIDTH - 1, D), 1.0),
        'state_cconv': nrm((N_CF, DEC_BATCH, CF_WIDTH - 1, D), 0.5),
        'cache_mem_k': nrm((DEPTH, DEC_BATCH, MEM_LEN, MEM_HEADS, MEM_HD), 1.0),
        'cache_mem_v': nrm((DEPTH, DEC_BATCH, MEM_LEN, MEM_HEADS, MEM_HD), 1.0),
        'mem_prompt': nrm((BATCH, MEM_LEN, D), 1.0),
        'norm_mix': gain((DEPTH, D)),
        'norm_mem': gain((DEPTH, D)),
        'norm_ffn': gain((DEPTH, D)),
        'norm_final': gain((D,)),
        'gla_wq': nrm((N_GLA, D, GLA_DK), D ** -0.5),
        'gla_wk': nrm((N_GLA, D, GLA_DK), D ** -0.5),
        'gla_wv': nrm((N_GLA, D, GLA_DV), D ** -0.5),
        'gla_wg1': nrm((N_GLA, D, GLA_GATE_RANK), D ** -0.5),
        'gla_wg2': nrm((N_GLA, GLA_GATE_RANK, GLA_DK), GLA_GATE_RANK ** -0.5),
        'gla_bg': nrm((N_GLA, GLA_DK), 0.1),
        'gla_wr': nrm((N_GLA, D, GLA_DV), D ** -0.5),
        'gla_norm': gain((N_GLA, GLA_DV)),
        'gla_wo': nrm((N_GLA, GLA_DV, D), GLA_DV ** -0.5),
        'sc_win': nrm((N_SC, D, 3 * D), D ** -0.5),
        'sc_wconv': nrm((N_SC, SC_WIDTH, D), SC_WIDTH ** -0.5),
        'sc_wout': nrm((N_SC, D, D), D ** -0.5),
        'cf_w1': nrm((N_CF, D, 2 * D), D ** -0.5),
        'cf_b1': nrm((N_CF, 2 * D), 0.02),
        'cf_wdw': nrm((N_CF, CF_WIDTH, D), CF_WIDTH ** -0.5),
        'cf_bdw': nrm((N_CF, D), 0.02),
        'cf_ln_g': gain((N_CF, D)),
        'cf_ln_b': nrm((N_CF, D), 0.02),
        'cf_w2': nrm((N_CF, D, D), D ** -0.5),
        'cf_b2': nrm((N_CF, D), 0.02),
        'mem_wq': nrm((DEPTH, D, MEM_DIM), D ** -0.5),
        'mem_wk': nrm((DEPTH, D, MEM_DIM), D ** -0.5),
        'mem_wv': nrm((DEPTH, D, MEM_DIM), D ** -0.5),
        'mem_wo': nrm((DEPTH, MEM_DIM, D), MEM_DIM ** -0.5),
        'ffn_up': nrm((DEPTH, D, D_FF), D ** -0.5),
        'ffn_down': nrm((DEPTH, D_FF, D), D_FF ** -0.5),
    }


def reference(x_prompt, x_sample, state_gla, state_sconv, state_cconv, cache_mem_k, cache_mem_v,
              mem_prompt, norm_mix, norm_mem, norm_ffn, norm_final,
              gla_wq, gla_wk, gla_wv, gla_wg1, gla_wg2, gla_bg, gla_wr, gla_norm, gla_wo,
              sc_win, sc_wconv, sc_wout,
              cf_w1, cf_b1, cf_wdw, cf_bdw, cf_ln_g, cf_ln_b, cf_w2, cf_b2,
              mem_wq, mem_wk, mem_wv, mem_wo, ffn_up, ffn_down):
    w = dict(norm_mix=norm_mix, norm_mem=norm_mem, norm_ffn=norm_ffn, norm_final=norm_final,
             gla_wq=gla_wq, gla_wk=gla_wk, gla_wv=gla_wv, gla_wg1=gla_wg1, gla_wg2=gla_wg2,
             gla_bg=gla_bg, gla_wr=gla_wr, gla_norm=gla_norm, gla_wo=gla_wo,
             sc_win=sc_win, sc_wconv=sc_wconv, sc_wout=sc_wout,
             cf_w1=cf_w1, cf_b1=cf_b1, cf_wdw=cf_wdw, cf_bdw=cf_bdw, cf_ln_g=cf_ln_g,
             cf_ln_b=cf_ln_b, cf_w2=cf_w2, cf_b2=cf_b2,
             mem_wq=mem_wq, mem_wo=mem_wo, ffn_up=ffn_up, ffn_down=ffn_down)

    bp = x_prompt.shape[0]
    mem_k_p, mem_v_p = _mem_kv(mem_prompt, mem_wk, mem_wv)
    gla0 = jnp.zeros((N_GLA, bp, GLA_HEADS, GLA_HK, GLA_HV), jnp.float32)
    sc0 = jnp.zeros((N_SC, bp, SC_WIDTH - 1, D_MODEL), x_prompt.dtype)
    cf0 = jnp.zeros((N_CF, bp, CF_WIDTH - 1, D_MODEL), x_prompt.dtype)
    y_prompt, gla_p, sc_p, cf_p = _trunk(x_prompt, mem_k_p, mem_v_p, gla0, sc0, cf0, w)

    y_sample, gla_s, sc_s, cf_s = _trunk(x_sample, cache_mem_k, cache_mem_v,
                                         state_gla, state_sconv, state_cconv, w)

    return (y_prompt, y_sample, gla_p, gla_s, sc_p, sc_s, cf_p, cf_s, mem_k_p, mem_v_p)
```

```python
import functools

import jax
import jax.numpy as jnp
from jax import lax
from jax.experimental import pallas as pl
from jax.experimental.pallas import tpu as pltpu

F32 = jnp.float32
BF16 = jnp.bfloat16

EPS = 1e-6
GLA_TAU = 16.0
GLA_SUB = 16
GLA_CHUNK = 64
V7X_VMEM_LIMIT_BYTES = 56 * 1024 * 1024
LANES = 128
MM_ROWS = 1024
ROW_TILES = (512, 256, 128, 64, 32, 16, 8)
SCONV_ROWS = 512
CCONV_ROWS = 128
ATTN_ROWS = 512


def _params(n_axes):
    return pltpu.CompilerParams(
        dimension_semantics=("arbitrary",) * n_axes,
        vmem_limit_bytes=V7X_VMEM_LIMIT_BYTES,
    )


def _dot(a, b):
    return jnp.dot(a, b, preferred_element_type=F32)


def _dot_nt(a, b):
    return lax.dot_general(a, b, (((1,), (1,)), ((), ())), preferred_element_type=F32)


def _dot_tn(a, b):
    return lax.dot_general(a, b, (((0,), (0,)), ((), ())), preferred_element_type=F32)


def _sigmoid(x):
    return 1.0 / (1.0 + jnp.exp(-x))


def _pick(n, prefs):
    for p in prefs:
        if p <= n and n % p == 0:
            return p
    return n


def _mm_body(*refs, groups, nk, kind, has_ss, has_bias, inv_d):
    it = iter(refs)
    a_ref = next(it)
    w_refs = [next(it) for _ in range(groups)]
    ss_ref = next(it) if has_ss else None
    b_refs = [next(it) for _ in range(groups)] if has_bias else None
    if kind == "delta":
        res_ref = next(it)
        gain_ref = next(it)
    n_out = {"plain": 1, "relu2": 1, "glu": 1, "sconv": 2, "delta": 3}[kind]
    out_refs = [next(it) for _ in range(n_out)]
    acc_refs = [next(it) for _ in range(groups)] if nk > 1 else None

    j = pl.program_id(1)
    k = pl.program_id(2)
    a = a_ref[...]
    if a.dtype != BF16:
        a = a.astype(BF16)
    parts = [_dot(a, w_ref[...].astype(BF16)) for w_ref in w_refs]

    def epilogue(accs):
        if has_ss:
            r = lax.rsqrt(ss_ref[...] * inv_d + EPS)
            accs = [x * r for x in accs]
        if has_bias:
            accs = [x + b_ref[...] for x, b_ref in zip(accs, b_refs)]
        if kind == "plain":
            out_refs[0][...] = accs[0].astype(out_refs[0].dtype)
        elif kind == "relu2":
            h = jnp.maximum(accs[0], 0.0)
            out_refs[0][...] = (h * h).astype(out_refs[0].dtype)
        elif kind == "glu":
            out_refs[0][...] = accs[0] * _sigmoid(accs[1])
        elif kind == "sconv":
            out_refs[0][...] = accs[0]
            out_refs[1][...] = accs[1] * accs[2]
        else:
            xn = res_ref[...] + accs[0]
            out_refs[0][...] = xn
            out_refs[1][...] = (xn * gain_ref[...]).astype(BF16)
            s = jnp.sum(xn * xn, axis=1, keepdims=True)

            @pl.when(j == 0)
            def _():
                out_refs[2][...] = s

            @pl.when(j > 0)
            def _():
                out_refs[2][...] += s

    if nk == 1:
        epilogue(parts)
    else:
        @pl.when(k == 0)
        def _():
            for acc_ref, p in zip(acc_refs, parts):
                acc_ref[...] = p

        @pl.when(k > 0)
        def _():
            for acc_ref, p in zip(acc_refs, parts):
                acc_ref[...] += p

        @pl.when(k == nk - 1)
        def _():
            epilogue([acc_ref[...] for acc_ref in acc_refs])


def _matmul(a, w, layer, *, kind, name, groups=1, ss=None, bias=None, bias_layer=0,
            res=None, gain=None, gain_layer=0, out_dtype=F32, tm=None, tn=512, tk=None):
    m, kdim = a.shape
    n = w.shape[-1] // groups
    tm = _pick(m, (MM_ROWS if tm is None else tm,) + ROW_TILES)
    tn = _pick(n, (tn, 512, 256, 128))
    tk = kdim if tk is None else _pick(kdim, (tk, 1024, 512, 256, 128))
    nk = kdim // tk
    nj = n // tn
    grid = (m // tm, nj, nk)
    has_ss = ss is not None
    has_bias = bias is not None

    in_specs = [pl.BlockSpec((tm, tk), lambda i, j, k: (i, k))]
    operands = [a]
    for g in range(groups):
        in_specs.append(pl.BlockSpec((None, tk, tn), lambda i, j, k, g=g: (layer, k, g * nj + j)))
        operands.append(w)
    if has_ss:
        in_specs.append(pl.BlockSpec((tm, 1), lambda i, j, k: (i, 0)))
        operands.append(ss)
    if has_bias:
        b3 = bias.reshape(bias.shape[0], 1, bias.shape[-1])
        for g in range(groups):
            in_specs.append(pl.BlockSpec((None, 1, tn), lambda i, j, k, g=g: (bias_layer, 0, g * nj + j)))
            operands.append(b3)
    tile = pl.BlockSpec((tm, tn), lambda i, j, k: (i, j))
    if kind == "delta":
        g3 = gain.reshape(-1, 1, gain.shape[-1])
        in_specs += [tile, pl.BlockSpec((None, 1, tn), lambda i, j, k: (gain_layer, 0, j))]
        operands += [res, g3]
        out_shape = [jax.ShapeDtypeStruct((m, n), F32), jax.ShapeDtypeStruct((m, n), BF16),
                     jax.ShapeDtypeStruct((m, 1), F32)]
        out_specs = [tile, tile, pl.BlockSpec((tm, 1), lambda i, j, k: (i, 0))]
    elif kind == "sconv":
        out_shape = [jax.ShapeDtypeStruct((m, n), F32)] * 2
        out_specs = [tile, tile]
    else:
        out_shape = [jax.ShapeDtypeStruct((m, n), out_dtype)]
        out_specs = [tile]
    scratch = [pltpu.VMEM((tm, tn), F32) for _ in range(groups)] if nk > 1 else []

    body = functools.partial(_mm_body, groups=groups, nk=nk, kind=kind, has_ss=has_ss,
                             has_bias=has_bias, inv_d=1.0 / kdim)
    outs = pl.pallas_call(
        body, out_shape=out_shape, grid=grid, in_specs=in_specs, out_specs=out_specs,
        scratch_shapes=scratch, compiler_params=_params(3), name=name,
    )(*operands)
    return outs if len(outs) > 1 else outs[0]


def _prep_body(xp_ref, xs_ref, g_ref, x_ref, xb_ref, ss_ref, *, n_prompt_tiles):
    i = pl.program_id(0)

    def emit(x):
        x_ref[...] = x
        xb_ref[...] = (x * g_ref[...]).astype(BF16)
        ss_ref[...] = jnp.sum(x * x, axis=1, keepdims=True)

    @pl.when(i < n_prompt_tiles)
    def _():
        emit(xp_ref[...])

    @pl.when(i >= n_prompt_tiles)
    def _():
        emit(xs_ref[...])


def _prep(xp2, xs2, gain, gain_layer):
    mp, d = xp2.shape
    ms = xs2.shape[0]
    tr = _pick(ms, (256, 128, 64, 32, 16, 8))
    npt, nst = mp // tr, ms // tr
    m = mp + ms
    g3 = gain.reshape(-1, 1, d)
    row = pl.BlockSpec((tr, d), lambda i: (i, 0))
    return pl.pallas_call(
        functools.partial(_prep_body, n_prompt_tiles=npt),
        out_shape=[jax.ShapeDtypeStruct((m, d), F32), jax.ShapeDtypeStruct((m, d), BF16),
                   jax.ShapeDtypeStruct((m, 1), F32)],
        grid=(npt + nst,),
        in_specs=[pl.BlockSpec((tr, d), lambda i: (jnp.minimum(i, npt - 1), 0)),
                  pl.BlockSpec((tr, d), lambda i: (jnp.maximum(i - npt, 0), 0)),
                  pl.BlockSpec((None, 1, d), lambda i: (gain_layer, 0, 0))],
        out_specs=[row, row, pl.BlockSpec((tr, 1), lambda i: (i, 0))],
        compiler_params=_params(1), name="prep",
    )(xp2, xs2, g3)


def _final_body(x_ref, ss_ref, g_ref, y_ref, *, inv_d):
    y_ref[...] = x_ref[...] * lax.rsqrt(ss_ref[...] * inv_d + EPS) * g_ref[...]


def _final_norm(x, ss, gain, row0, rows, name):
    d = x.shape[1]
    tr = _pick(rows, (256, 128, 64, 32, 16, 8))
    off = row0 // tr
    return pl.pallas_call(
        functools.partial(_final_body, inv_d=1.0 / d),
        out_shape=jax.ShapeDtypeStruct((rows, d), F32),
        grid=(rows // tr,),
        in_specs=[pl.BlockSpec((tr, d), lambda i: (off + i, 0)),
                  pl.BlockSpec((tr, 1), lambda i: (off + i, 0)),
                  pl.BlockSpec((1, d), lambda i: (0, 0))],
        out_specs=pl.BlockSpec((tr, d), lambda i: (i, 0)),
        compiler_params=_params(1), name=name,
    )(x, ss, gain.reshape(1, d))


def _log_sigmoid(x):
    return jnp.minimum(x, 0.0) - jnp.log1p(jnp.exp(-jnp.abs(x)))


def _split3(x):
    hi = x.astype(BF16)
    r1 = x - hi.astype(F32)
    mid = r1.astype(BF16)
    lo = (r1 - mid.astype(F32)).astype(BF16)
    return hi, mid, lo


def _gla_chunk(q, k, v, g1, wg2, bg, s, *, sub, scale):
    c, hk = q.shape
    g = _log_sigmoid(_dot(g1.astype(BF16), wg2.astype(BF16)) + bg) * (1.0 / GLA_TAU)

    row = lax.broadcasted_iota(jnp.int32, (c, c), 0)
    col = lax.broadcasted_iota(jnp.int32, (c, c), 1)
    causal = col <= row
    tril = jnp.where(causal, 1.0, 0.0).astype(BF16)
    g_parts = _split3(g)
    b = _dot(tril, g_parts[0]) + _dot(tril, g_parts[1]) + _dot(tril, g_parts[2])
    ones = jnp.ones((c, LANES), BF16)
    b_last_col = (_dot_tn(g_parts[0], ones) + _dot_tn(g_parts[1], ones)
                  + _dot_tn(g_parts[2], ones))[:, :1]
    b_last = b[c - 1:c, :]

    qs = q * scale
    vb = v.astype(BF16)
    o = _dot((qs * jnp.exp(b)).astype(BF16), s.astype(BF16))

    key_pos = lax.broadcasted_iota(jnp.int32, (c, 1), 0)
    att_rows = []
    for i in range(c // sub):
        lo, hi = i * sub, (i + 1) * sub
        b_ref = b[lo - 1:lo, :] if i > 0 else jnp.zeros((1, hk), F32)
        qi = (qs[lo:hi] * jnp.exp(b[lo:hi] - b_ref)).astype(BF16)
        expo = jnp.where(key_pos < hi, b_ref - b, -1e30)
        kk = (k * jnp.exp(expo)).astype(BF16)
        att_rows.append(_dot_nt(qi, kk))
    att = att_rows[0] if len(att_rows) == 1 else jnp.concatenate(att_rows, axis=0)
    att = jnp.where(causal, att, 0.0)
    o = o + _dot(att.astype(BF16), vb)

    kd = (k * jnp.exp(b_last - b)).astype(BF16)
    s_new = jnp.exp(b_last_col) * s + _dot_tn(kd, vb)
    return o, s_new


def _gla_gate_out(o, rg, ng):
    on = o * lax.rsqrt(jnp.mean(o * o, axis=-1, keepdims=True) + EPS) * ng
    return (on * (rg * _sigmoid(rg))).astype(BF16)


def _gla_prompt_body(q_ref, k_ref, v_ref, g1_ref, wg2_ref, bg_ref, rg_ref, ng_ref,
                     og_ref, sout_ref, s_ref, *, nc, scale):
    c = pl.program_id(2)

    @pl.when(c == 0)
    def _():
        s_ref[...] = jnp.zeros_like(s_ref)

    o, s_new = _gla_chunk(q_ref[...], k_ref[...], v_ref[...], g1_ref[...], wg2_ref[...], bg_ref[...],
                          s_ref[...], sub=GLA_SUB, scale=scale)
    s_ref[...] = s_new
    og_ref[...] = _gla_gate_out(o, rg_ref[...], ng_ref[...])

    @pl.when(c == nc - 1)
    def _():
        sout_ref[...] = s_new


def _gla_sample_body(q_ref, k_ref, v_ref, g1_ref, wg2_ref, bg_ref, rg_ref, ng_ref, s0_ref, *rest,
                     scale, sub):
    og_ref, sout_ref = rest[-2], rest[-1]
    o, s_new = _gla_chunk(q_ref[...], k_ref[...], v_ref[...], g1_ref[...], wg2_ref[...], bg_ref[...],
                          s0_ref[...], sub=sub, scale=scale)
    og_ref[...] = _gla_gate_out(o, rg_ref[...], ng_ref[...])
    sout_ref[...] = s_new


def _gla_scan(q, k, v, g1, rg, wg2p, bg, ng, state_in, layer_j, sp_prev, ss_prev, *,
              n_prompt, l_prompt, n_sample, l_sample):
    n_gla, _, nh, hk, hv = state_in.shape
    m = q.shape[0]
    scale = float(hk) ** -0.5
    rk = g1.shape[1]
    wg3 = wg2p
    bg3 = bg.reshape(n_gla, 1, nh * hk)
    ng3 = ng.reshape(n_gla, 1, nh * hv)

    cc = _pick(l_prompt, (GLA_CHUNK, 32, 16, 8))
    nc = l_prompt // cc
    sp_shape = jax.ShapeDtypeStruct((n_gla, n_prompt, nh, hk, hv), F32)
    rowsel = lambda b, h, c: (b * nc + c, h)
    in_specs = [pl.BlockSpec((cc, hk), rowsel), pl.BlockSpec((cc, hk), rowsel),
                pl.BlockSpec((cc, hv), rowsel),
                pl.BlockSpec((cc, rk), lambda b, h, c: (b * nc + c, 0)),
                pl.BlockSpec((None, rk, hk), lambda b, h, c: (layer_j, 0, h)),
                pl.BlockSpec((None, 1, hk), lambda b, h, c: (layer_j, 0, h)),
                pl.BlockSpec((cc, hv), rowsel),
                pl.BlockSpec((None, 1, hv), lambda b, h, c: (layer_j, 0, h))]
    operands = [q, k, v, g1, wg3, bg3, rg, ng3]
    aliases = {}
    if sp_prev is not None:
        in_specs.append(pl.BlockSpec(memory_space=pl.ANY))
        operands.append(sp_prev)
        aliases = {len(operands) - 1: 1}

    def prompt_body(*refs):
        if sp_prev is not None:
            refs = refs[:8] + refs[9:]
        _gla_prompt_body(*refs, nc=nc, scale=scale)

    og, sp = pl.pallas_call(
        prompt_body,
        out_shape=[jax.ShapeDtypeStruct((m, nh * hv), BF16), sp_shape],
        grid=(n_prompt, nh, nc), in_specs=in_specs,
        out_specs=[pl.BlockSpec((cc, hv), rowsel),
                   pl.BlockSpec((None, None, None, hk, hv), lambda b, h, c: (layer_j, b, h, 0, 0))],
        scratch_shapes=[pltpu.VMEM((hk, hv), F32)],
        input_output_aliases=aliases, compiler_params=_params(3), name="gla_prompt",
    )(*operands)

    r0 = (n_prompt * l_prompt) // l_sample
    rowsel_s = lambda b, h: (r0 + b, h)
    in_specs = [pl.BlockSpec((l_sample, hk), rowsel_s), pl.BlockSpec((l_sample, hk), rowsel_s),
                pl.BlockSpec((l_sample, hv), rowsel_s),
                pl.BlockSpec((l_sample, rk), lambda b, h: (r0 + b, 0)),
                pl.BlockSpec((None, rk, hk), lambda b, h: (layer_j, 0, h)),
                pl.BlockSpec((None, 1, hk), lambda b, h: (layer_j, 0, h)),
                pl.BlockSpec((l_sample, hv), rowsel_s),
                pl.BlockSpec((None, 1, hv), lambda b, h: (layer_j, 0, h)),
                pl.BlockSpec((None, None, None, hk, hv), lambda b, h: (layer_j, b, h, 0, 0)),
                pl.BlockSpec(memory_space=pl.ANY)]
    operands = [q, k, v, g1, wg3, bg3, rg, ng3, state_in, og]
    aliases = {9: 0}
    if ss_prev is not None:
        in_specs.append(pl.BlockSpec(memory_space=pl.ANY))
        operands.append(ss_prev)
        aliases[10] = 1
    og, ss_new = pl.pallas_call(
        functools.partial(_gla_sample_body, scale=scale, sub=min(GLA_SUB, l_sample)),
        out_shape=[jax.ShapeDtypeStruct((m, nh * hv), BF16),
                   jax.ShapeDtypeStruct((n_gla, n_sample, nh, hk, hv), F32)],
        grid=(n_sample, nh), in_specs=in_specs,
        out_specs=[pl.BlockSpec((l_sample, hv), rowsel_s),
                   pl.BlockSpec((None, None, None, hk, hv), lambda b, h: (layer_j, b, h, 0, 0))],
        input_output_aliases=aliases, compiler_params=_params(2), name="gla_sample",
    )(*operands)
    return og, sp, ss_new


def _softmax_rows(s):
    e = jnp.exp(s - jnp.max(s, axis=-1, keepdims=True))
    return e / jnp.sum(e, axis=-1, keepdims=True)


def _attn_prompt_body(q_ref, k_ref, v_ref, o_ref, *, nh, hd, scale):
    q = q_ref[...].astype(BF16)
    k = k_ref[...].astype(BF16)
    v = v_ref[...].astype(BF16)
    outs = []
    for h in range(nh):
        sl = slice(h * hd, (h + 1) * hd)
        p = _softmax_rows(_dot_nt(q[:, sl], k[:, sl]) * scale)
        outs.append(_dot(p.astype(BF16), v[:, sl]))
    o_ref[...] = jnp.concatenate(outs, axis=1).astype(BF16)


def _attn_sample_body(q_ref, k_ref, v_ref, o_in_ref, o_ref, *, nh, hd, scale, bb, ls):
    del o_in_ref
    q = q_ref[...].reshape(bb, ls, nh * hd).astype(BF16)
    k = k_ref[...].astype(BF16)
    v = v_ref[...].astype(BF16)
    outs = []
    for h in range(nh):
        sl = slice(h * hd, (h + 1) * hd)
        s = lax.dot_general(q[:, :, sl], k[:, :, sl], (((2,), (2,)), ((0,), (0,))),
                            preferred_element_type=F32) * scale
        p = _softmax_rows(s).astype(BF16)
        outs.append(lax.dot_general(p, v[:, :, sl], (((2,), (1,)), ((0,), (0,))),
                                    preferred_element_type=F32))
    o = jnp.concatenate(outs, axis=2)
    o_ref[...] = o.reshape(bb * ls, nh * hd).astype(BF16)


def _mem_attention(q, kp, vp, cache_k, cache_v, layer, *, n_prompt, l_prompt, n_sample, l_sample,
                   nh, hd):
    m, dm = q.shape
    mem_len = cache_k.shape[2]
    scale = float(hd) ** -0.5
    tq = _pick(l_prompt, (ATTN_ROWS,) + ROW_TILES)
    nt = l_prompt // tq
    o = pl.pallas_call(
        functools.partial(_attn_prompt_body, nh=nh, hd=hd, scale=scale),
        out_shape=jax.ShapeDtypeStruct((m, dm), BF16),
        grid=(n_prompt, nt),
        in_specs=[pl.BlockSpec((tq, dm), lambda b, t: (b * nt + t, 0)),
                  pl.BlockSpec((mem_len, dm), lambda b, t: (b, 0)),
                  pl.BlockSpec((mem_len, dm), lambda b, t: (b, 0))],
        out_specs=pl.BlockSpec((tq, dm), lambda b, t: (b * nt + t, 0)),
        compiler_params=_params(2), name="attn_prompt",
    )(q, kp, vp)

    bb = _pick(n_sample, (8, 4, 2, 1))
    r0 = (n_prompt * l_prompt) // (bb * l_sample)
    o = pl.pallas_call(
        functools.partial(_attn_sample_body, nh=nh, hd=hd, scale=scale, bb=bb, ls=l_sample),
        out_shape=jax.ShapeDtypeStruct((m, dm), BF16),
        grid=(n_sample // bb,),
        in_specs=[pl.BlockSpec((bb * l_sample, dm), lambda i: (r0 + i, 0)),
                  pl.BlockSpec((None, bb, mem_len, dm), lambda i: (layer, i, 0, 0)),
                  pl.BlockSpec((None, bb, mem_len, dm), lambda i: (layer, i, 0, 0)),
                  pl.BlockSpec(memory_space=pl.ANY)],
        out_specs=pl.BlockSpec((bb * l_sample, dm), lambda i: (r0 + i, 0)),
        input_output_aliases={3: 0}, compiler_params=_params(1), name="attn_sample",
    )(q, cache_k, cache_v, o)
    return o


def _shift_rows(u, hist_rows, t, axis):
    n = len(hist_rows)
    out = []
    for s in range(1, n + 1):
        shifted = pltpu.roll(u, s, axis=axis)
        for tt in range(s):
            shifted = jnp.where(t == tt, hist_rows[n - s + tt], shifted)
        out.append(shifted)
    return out


def _sconv_prompt_body(u_ref, prev_ref, bg_ref, w_ref, z_ref, *, tiles_per_seq):
    i = pl.program_id(0)
    u = u_ref[...]
    keep = jnp.where(i % tiles_per_seq == 0, 0.0, 1.0)
    prev = prev_ref[...] * keep
    kw = w_ref.shape[0]
    hist = [prev[8 - (kw - 1) + r: 8 - (kw - 1) + r + 1, :] for r in range(kw - 1)]
    t = lax.broadcasted_iota(jnp.int32, (u.shape[0], 1), 0)
    shifted = _shift_rows(u, hist, t, 0)
    y = w_ref[kw - 1:kw, :] * u
    for s, us in enumerate(shifted, start=1):
        y = y + w_ref[kw - 1 - s:kw - s, :] * us
    z_ref[...] = (bg_ref[...] * y).astype(BF16)


def _sconv_sample_body(u_ref, hist_ref, bg_ref, w_ref, z_in_ref, z_ref, *, bb, ls):
    del z_in_ref
    tc = u_ref.shape[1]
    u = u_ref[...].reshape(bb, ls, tc)
    kw = w_ref.shape[0]
    hist = [hist_ref[:, r:r + 1, :] for r in range(kw - 1)]
    t = lax.broadcasted_iota(jnp.int32, (1, ls, 1), 1)
    shifted = _shift_rows(u, hist, t, 1)
    y = w_ref[kw - 1:kw, :] * u
    for s, us in enumerate(shifted, start=1):
        y = y + w_ref[kw - 1 - s:kw - s, :] * us
    z_ref[...] = (bg_ref[...] * y.reshape(bb * ls, tc)).astype(BF16)


def _short_conv(u, bgate, wconv, layer_j, state, *, n_prompt, l_prompt, n_sample, l_sample):
    m, d = u.shape
    kw = wconv.shape[1]
    tr = _pick(l_prompt, (SCONV_ROWS,) + ROW_TILES)
    tc = _pick(d, (512, 256, 128))
    tps = l_prompt // tr
    z = pl.pallas_call(
        functools.partial(_sconv_prompt_body, tiles_per_seq=tps),
        out_shape=jax.ShapeDtypeStruct((m, d), BF16),
        grid=(n_prompt * tps, d // tc),
        in_specs=[pl.BlockSpec((tr, tc), lambda i, j: (i, j)),
                  pl.BlockSpec((8, tc), lambda i, j: (jnp.maximum(i * (tr // 8) - 1, 0), j)),
                  pl.BlockSpec((tr, tc), lambda i, j: (i, j)),
                  pl.BlockSpec((None, kw, tc), lambda i, j: (layer_j, 0, j))],
        out_specs=pl.BlockSpec((tr, tc), lambda i, j: (i, j)),
        compiler_params=_params(2), name="sconv_prompt",
    )(u, u, bgate, wconv)

    bb = _pick(n_sample, (64, 32, 16, 8, 4, 2, 1))
    r0 = (n_prompt * l_prompt) // (bb * l_sample)
    z = pl.pallas_call(
        functools.partial(_sconv_sample_body, bb=bb, ls=l_sample),
        out_shape=jax.ShapeDtypeStruct((m, d), BF16),
        grid=(n_sample // bb, d // tc),
        in_specs=[pl.BlockSpec((bb * l_sample, tc), lambda i, j: (r0 + i, j)),
                  pl.BlockSpec((None, bb, kw - 1, tc), lambda i, j: (layer_j, i, 0, j)),
                  pl.BlockSpec((bb * l_sample, tc), lambda i, j: (r0 + i, j)),
                  pl.BlockSpec((None, kw, tc), lambda i, j: (layer_j, 0, j)),
                  pl.BlockSpec(memory_space=pl.ANY)],
        out_specs=pl.BlockSpec((bb * l_sample, tc), lambda i, j: (r0 + i, j)),
        input_output_aliases={4: 0}, compiler_params=_params(2), name="sconv_sample",
    )(u, state, bgate, wconv, z)
    return z


def _cconv_core(u3, hist3, w_ref, bdw_ref, lg_ref, lb_ref, ext_ref, z_ref, out_ref, *, tc):
    j = pl.program_id(1)
    nj = pl.num_programs(1)
    bb, ll, _ = u3.shape
    hb = hist3.shape[1]
    kw = w_ref.shape[0]
    off = (-hb) % 8
    ext_ref[:, off:off + hb, :] = hist3
    ext_ref[:, off + hb:off + hb + ll, :] = u3
    base = off + hb - (kw - 1)
    acc = jnp.zeros((bb, ll, tc), F32) + bdw_ref[...]
    for kk in range(kw):
        acc = acc + w_ref[kk:kk + 1, :] * ext_ref[:, base + kk:base + kk + ll, :]
    z_ref[j] = acc

    @pl.when(j == nj - 1)
    def _():
        n_tiles = z_ref.shape[0]
        inv_d = 1.0 / (n_tiles * tc)
        z = z_ref[...]
        mu = jnp.sum(jnp.sum(z, axis=3, keepdims=True), axis=0, keepdims=True) * inv_d
        zc = z - mu
        var = jnp.sum(jnp.sum(zc * zc, axis=3, keepdims=True), axis=0, keepdims=True) * inv_d
        zn = zc * lax.rsqrt(var + EPS)
        for jj in range(n_tiles):
            cols = slice(jj * tc, (jj + 1) * tc)
            y = zn[jj] * lg_ref[:, cols] + lb_ref[:, cols]
            y = y * _sigmoid(y)
            out_ref[:, cols] = y.reshape(bb * ll, tc).astype(BF16)


def _cconv_prompt_body(u_ref, halo_ref, w_ref, bdw_ref, lg_ref, lb_ref, out_ref, ext_ref, z_ref, *,
                       tiles_per_seq, tc):
    i = pl.program_id(0)
    keep = jnp.where(i % tiles_per_seq == 0, 0.0, 1.0)
    u3 = u_ref[...][None]
    hist3 = (halo_ref[...] * keep)[None]
    _cconv_core(u3, hist3, w_ref, bdw_ref, lg_ref, lb_ref, ext_ref, z_ref, out_ref, tc=tc)


def _cconv_sample_body(u_ref, hist_ref, w_ref, bdw_ref, lg_ref, lb_ref, o_in_ref, out_ref, ext_ref,
                       z_ref, *, bb, ls, tc):
    del o_in_ref
    u3 = u_ref[...].reshape(bb, ls, tc)
    _cconv_core(u3, hist_ref[...], w_ref, bdw_ref, lg_ref, lb_ref, ext_ref, z_ref, out_ref, tc=tc)


def _conformer_conv(u, wdw, bdw, lng, lnb, layer_j, state, *, n_prompt, l_prompt, n_sample, l_sample):
    m, d = u.shape
    kw = wdw.shape[1]
    halo = 8 * ((kw - 1 + 7) // 8)
    tr = _pick(l_prompt, (CCONV_ROWS, 64, 32))
    tc = _pick(d, (512, 256, 128))
    tps = l_prompt // tr
    b3 = bdw.reshape(-1, 1, d)
    g3 = lng.reshape(-1, 1, d)
    l3 = lnb.reshape(-1, 1, d)
    vec = lambda i, j: (layer_j, 0, j)
    full = lambda i, j: (layer_j, 0, 0)
    zc = pl.pallas_call(
        functools.partial(_cconv_prompt_body, tiles_per_seq=tps, tc=tc),
        out_shape=jax.ShapeDtypeStruct((m, d), BF16),
        grid=(n_prompt * tps, d // tc),
        in_specs=[pl.BlockSpec((tr, tc), lambda i, j: (i, j)),
                  pl.BlockSpec((halo, tc), lambda i, j: (jnp.maximum(i * (tr // halo) - 1, 0), j)),
                  pl.BlockSpec((None, kw, tc), vec), pl.BlockSpec((None, 1, tc), vec),
                  pl.BlockSpec((None, 1, d), full), pl.BlockSpec((None, 1, d), full)],
        out_specs=pl.BlockSpec((tr, d), lambda i, j: (i, 0)),
        scratch_shapes=[pltpu.VMEM((1, tr + halo, tc), F32), pltpu.VMEM((d // tc, 1, tr, tc), F32)],
        compiler_params=_params(2), name="cconv_prompt",
    )(u, u, wdw, b3, g3, l3)

    bb = _pick(n_sample, (8, 4, 2, 1))
    r0 = (n_prompt * l_prompt) // (bb * l_sample)
    hb = kw - 1
    ext_rows = 8 * (((-hb) % 8 + hb + l_sample + 7) // 8)
    zc = pl.pallas_call(
        functools.partial(_cconv_sample_body, bb=bb, ls=l_sample, tc=tc),
        out_shape=jax.ShapeDtypeStruct((m, d), BF16),
        grid=(n_sample // bb, d // tc),
        in_specs=[pl.BlockSpec((bb * l_sample, tc), lambda i, j: (r0 + i, j)),
                  pl.BlockSpec((None, bb, hb, tc), lambda i, j: (layer_j, i, 0, j)),
                  pl.BlockSpec((None, kw, tc), vec), pl.BlockSpec((None, 1, tc), vec),
                  pl.BlockSpec((None, 1, d), full), pl.BlockSpec((None, 1, d), full),
                  pl.BlockSpec(memory_space=pl.ANY)],
        out_specs=pl.BlockSpec((bb * l_sample, d), lambda i, j: (r0 + i, 0)),
        scratch_shapes=[pltpu.VMEM((bb, ext_rows, tc), F32),
                        pltpu.VMEM((d // tc, bb, l_sample, tc), F32)],
        input_output_aliases={6: 0}, compiler_params=_params(2), name="cconv_sample",
    )(u, state, wdw, b3, g3, l3, zc)
    return zc


def kernel(x_prompt, x_sample, state_gla, state_sconv, state_cconv, cache_mem_k, cache_mem_v, mem_prompt, norm_mix, norm_mem, norm_ffn, norm_final, gla_wq, gla_wk, gla_wv, gla_wg1, gla_wg2, gla_bg, gla_wr, gla_norm, gla_wo, sc_win, sc_wconv, sc_wout, cf_w1, cf_b1, cf_wdw, cf_bdw, cf_ln_g, cf_ln_b, cf_w2, cf_b2, mem_wq, mem_wk, mem_wv, mem_wo, ffn_up, ffn_down):
    n_prompt, l_prompt, d = x_prompt.shape
    n_sample, l_sample, _ = x_sample.shape
    depth = norm_mix.shape[0]
    n_mixers = 3
    mp, ms = n_prompt * l_prompt, n_sample * l_sample
    seq = dict(n_prompt=n_prompt, l_prompt=l_prompt, n_sample=n_sample, l_sample=l_sample)
    _, _, mem_len, mem_heads, mem_hd = cache_mem_k.shape
    mem_dim = mem_heads * mem_hd
    n_gla, _, gla_heads, hk, hv = state_gla.shape
    kw_sc = sc_wconv.shape[1]
    kw_cf = cf_wdw.shape[1]

    mem2 = mem_prompt.reshape(n_prompt * mem_len, d)
    kps = [_matmul(mem2, mem_wk, i, kind="plain", name="mem_k", tm=512) for i in range(depth)]
    vps = [_matmul(mem2, mem_wv, i, kind="plain", name="mem_v", tm=512) for i in range(depth)]
    cache_k4 = cache_mem_k.reshape(depth, n_sample, mem_len, mem_dim)
    cache_v4 = cache_mem_v.reshape(depth, n_sample, mem_len, mem_dim)

    rank = gla_wg1.shape[-1]
    rpad = LANES * ((rank + LANES - 1) // LANES)
    wg1p = jnp.pad(gla_wg1, ((0, 0), (0, 0), (0, rpad - rank)))
    wg2p = jnp.pad(gla_wg2, ((0, 0), (0, rpad - rank), (0, 0)))

    x, xb, ss = _prep(x_prompt.reshape(mp, d), x_sample.reshape(ms, d), norm_mix, 0)

    gla_p = gla_s = None
    sc_us, cf_us = [], []
    for i in range(depth):
        kind, j = i % n_mixers, i // n_mixers
        if kind == 0:
            q = _matmul(xb, gla_wq, j, kind="plain", name="gla_q", ss=ss)
            k = _matmul(xb, gla_wk, j, kind="plain", name="gla_k", ss=ss)
            v = _matmul(xb, gla_wv, j, kind="plain", name="gla_v", ss=ss)
            rg = _matmul(xb, gla_wr, j, kind="plain", name="gla_r", ss=ss)
            g1 = _matmul(xb, wg1p, j, kind="plain", name="gla_g1", ss=ss, tn=LANES)
            mix, gla_p, gla_s = _gla_scan(q, k, v, g1, rg, wg2p, gla_bg, gla_norm, state_gla, j,
                                          gla_p, gla_s, **seq)
            x, xb, ss = _matmul(mix, gla_wo, j, kind="delta", name="gla_o", res=x,
                                gain=norm_mem, gain_layer=i)
        elif kind == 1:
            bgate, u = _matmul(xb, sc_win, j, kind="sconv", name="sc_in", groups=3, ss=ss, tn=256)
            sc_us.append(u)
            z = _short_conv(u, bgate, sc_wconv, j, state_sconv, **seq)
            x, xb, ss = _matmul(z, sc_wout, j, kind="delta", name="sc_out", res=x,
                                gain=norm_mem, gain_layer=i)
        else:
            u = _matmul(xb, cf_w1, j, kind="glu", name="cf_in", groups=2, ss=ss, bias=cf_b1,
                        bias_layer=j, tn=256)
            cf_us.append(u)
            zc = _conformer_conv(u, cf_wdw, cf_bdw, cf_ln_g, cf_ln_b, j, state_cconv, **seq)
            x, xb, ss = _matmul(zc, cf_w2, j, kind="delta", name="cf_out", res=x, bias=cf_b2,
                                bias_layer=j, gain=norm_mem, gain_layer=i)

        q = _matmul(xb, mem_wq, i, kind="plain", name="mem_q", ss=ss)
        o = _mem_attention(q, kps[i], vps[i], cache_k4, cache_v4, i, nh=mem_heads, hd=mem_hd, **seq)
        x, xb, ss = _matmul(o, mem_wo, i, kind="delta", name="mem_o", res=x, gain=norm_ffn,
                            gain_layer=i, tn=1024)

        hid = _matmul(xb, ffn_up, i, kind="relu2", name="ffn_up", ss=ss, out_dtype=BF16)
        last = i == depth - 1
        x, xb, ss = _matmul(hid, ffn_down, i, kind="delta", name="ffn_down", res=x,
                            gain=norm_final.reshape(1, d) if last else norm_mix,
                            gain_layer=0 if last else i + 1, tn=1024, tk=1024)

    y_prompt = _final_norm(x, ss, norm_final, 0, mp, "final_prompt").reshape(n_prompt, l_prompt, d)
    y_sample = _final_norm(x, ss, norm_final, mp, ms, "final_sample").reshape(n_sample, l_sample, d)

    def tails(us_, states, keep):
        tails_p, tails_s = [], []
        for jj, u in enumerate(us_):
            up = u[:mp].reshape(n_prompt, l_prompt, d)
            us = u[mp:].reshape(n_sample, l_sample, d)
            tails_p.append(jnp.concatenate([jnp.zeros((n_prompt, keep, d), F32), up], axis=1)[:, -keep:])
            tails_s.append(jnp.concatenate([states[jj], us], axis=1)[:, -keep:])
        return jnp.stack(tails_p), jnp.stack(tails_s)

    sc_p, sc_s = tails(sc_us, state_sconv, kw_sc - 1)
    cf_p, cf_s = tails(cf_us, state_cconv, kw_cf - 1)
    mem_k_p = jnp.stack(kps).reshape(depth, n_prompt, mem_len, mem_heads, mem_hd)
    mem_v_p = jnp.stack(vps).reshape(depth, n_prompt, mem_len, mem_heads, mem_hd)
    return (y_prompt, y_sample, gla_p, gla_s, sc_p, sc_s, cf_p, cf_s, mem_k_p, mem_v_p)
```

```python
import functools

import jax
import jax.numpy as jnp
from jax import lax
from jax.experimental import pallas as pl
from jax.experimental.pallas import tpu as pltpu

F32 = jnp.float32
BF16 = jnp.bfloat16

EPS = 1e-6
GLA_TAU = 16.0
GLA_SUB = 16
GLA_CHUNK = 128
V7X_VMEM_LIMIT_BYTES = 56 * 1024 * 1024
LANES = 128
MM_ROWS = 1024
FFN_DOWN_ROWS = 1536
ROW_TILES = (512, 256, 128, 64, 32, 16, 8)
SCONV_ROWS = 512
CCONV_ROWS = 128
ATTN_ROWS = 512


def _params(n_axes):
    return pltpu.CompilerParams(
        dimension_semantics=("arbitrary",) * n_axes,
        vmem_limit_bytes=V7X_VMEM_LIMIT_BYTES,
    )


def _dot(a, b):
    return jnp.dot(a, b, preferred_element_type=F32)


def _dot_nt(a, b):
    return lax.dot_general(a, b, (((1,), (1,)), ((), ())), preferred_element_type=F32)


def _dot_tn(a, b):
    return lax.dot_general(a, b, (((0,), (0,)), ((), ())), preferred_element_type=F32)


def _sigmoid(x):
    return 1.0 / (1.0 + jnp.exp(-x))


def _pick(n, prefs):
    for p in prefs:
        if p <= n and n % p == 0:
            return p
    return n


def _mm_body(*refs, groups, nk, kind, has_ss, has_bias, inv_d):
    it = iter(refs)
    a_ref = next(it)
    w_refs = [next(it) for _ in range(groups)]
    ss_ref = next(it) if has_ss else None
    b_refs = [next(it) for _ in range(groups)] if has_bias else None
    if kind == "delta":
        res_ref = next(it)
        gain_ref = next(it)
    n_out = {"plain": 1, "relu2": 1, "glu": 1, "sconv": 2, "delta": 3}[kind]
    out_refs = [next(it) for _ in range(n_out)]
    acc_refs = [next(it) for _ in range(groups)] if nk > 1 and kind != "delta" else None

    j = pl.program_id(1)
    k = pl.program_id(2)

    def products():
        a = a_ref[...]
        if a.dtype != BF16:
            a = a.astype(BF16)
        return [_dot(a, w_ref[...].astype(BF16)) for w_ref in w_refs]

    def epilogue(accs):
        if has_ss:
            r = lax.rsqrt(ss_ref[...] * inv_d + EPS)
            accs = [x * r for x in accs]
        if has_bias:
            accs = [x + b_ref[...] for x, b_ref in zip(accs, b_refs)]
        if kind == "plain":
            out_refs[0][...] = accs[0].astype(out_refs[0].dtype)
        elif kind == "relu2":
            h = jnp.maximum(accs[0], 0.0)
            out_refs[0][...] = (h * h).astype(out_refs[0].dtype)
        elif kind == "glu":
            out_refs[0][...] = accs[0] * _sigmoid(accs[1])
        elif kind == "sconv":
            out_refs[0][...] = accs[0]
            out_refs[1][...] = accs[1] * accs[2]
        else:
            xn = res_ref[...] + accs[0]
            out_refs[0][...] = xn
            delta_tail(xn)

    def delta_tail(xn):
        out_refs[1][...] = (xn * gain_ref[...]).astype(BF16)
        s = jnp.sum(xn * xn, axis=1, keepdims=True)

        @pl.when(j == 0)
        def _():
            out_refs[2][...] = s

        @pl.when(j > 0)
        def _():
            out_refs[2][...] += s

    if nk == 1:
        epilogue(products())
    elif kind == "delta":
        @pl.when(k == 0)
        def _():
            seed = res_ref[...]
            if has_bias:
                seed = seed + b_refs[0][...]
            out_refs[0][...] = seed

        out_refs[0][...] = out_refs[0][...] + products()[0]

        @pl.when(k == nk - 1)
        def _():
            delta_tail(out_refs[0][...])
    else:
        @pl.when(k == 0)
        def _():
            for acc_ref in acc_refs:
                acc_ref[...] = jnp.zeros_like(acc_ref)

        for acc_ref, p in zip(acc_refs, products()):
            acc_ref[...] = acc_ref[...] + p

        @pl.when(k == nk - 1)
        def _():
            epilogue([acc_ref[...] for acc_ref in acc_refs])


def _matmul(a, w, layer, *, kind, name, groups=1, ss=None, bias=None, bias_layer=0,
            res=None, gain=None, gain_layer=0, out_dtype=F32, tm=None, tn=512, tk=None):
    m, kdim = a.shape
    n = w.shape[-1] // groups
    tm = _pick(m, (MM_ROWS if tm is None else tm,) + ROW_TILES)
    tn = _pick(n, (tn, 512, 256, 128))
    tk = kdim if tk is None else _pick(kdim, (tk, 1024, 512, 256, 128))
    nk = kdim // tk
    nj = n // tn
    grid = (m // tm, nj, nk)
    has_ss = ss is not None
    has_bias = bias is not None

    in_specs = [pl.BlockSpec((tm, tk), lambda i, j, k: (i, k))]
    operands = [a]
    for g in range(groups):
        in_specs.append(pl.BlockSpec((None, tk, tn), lambda i, j, k, g=g: (layer, k, g * nj + j)))
        operands.append(w)
    if has_ss:
        in_specs.append(pl.BlockSpec((tm, 1), lambda i, j, k: (i, 0)))
        operands.append(ss)
    if has_bias:
        b3 = bias.reshape(bias.shape[0], 1, bias.shape[-1])
        for g in range(groups):
            in_specs.append(pl.BlockSpec((None, 1, tn), lambda i, j, k, g=g: (bias_layer, 0, g * nj + j)))
            operands.append(b3)
    tile = pl.BlockSpec((tm, tn), lambda i, j, k: (i, j))
    if kind == "delta":
        g3 = gain.reshape(-1, 1, gain.shape[-1])
        in_specs += [tile, pl.BlockSpec((None, 1, tn), lambda i, j, k: (gain_layer, 0, j))]
        operands += [res, g3]
        out_shape = [jax.ShapeDtypeStruct((m, n), F32), jax.ShapeDtypeStruct((m, n), BF16),
                     jax.ShapeDtypeStruct((m, 1), F32)]
        out_specs = [tile, tile, pl.BlockSpec((tm, 1), lambda i, j, k: (i, 0))]
    elif kind == "sconv":
        out_shape = [jax.ShapeDtypeStruct((m, n), F32)] * 2
        out_specs = [tile, tile]
    else:
        out_shape = [jax.ShapeDtypeStruct((m, n), out_dtype)]
        out_specs = [tile]
    scratch = [pltpu.VMEM((tm, tn), F32) for _ in range(groups)] if nk > 1 and kind != "delta" else []

    body = functools.partial(_mm_body, groups=groups, nk=nk, kind=kind, has_ss=has_ss,
                             has_bias=has_bias, inv_d=1.0 / kdim)
    outs = pl.pallas_call(
        body, out_shape=out_shape, grid=grid, in_specs=in_specs, out_specs=out_specs,
        scratch_shapes=scratch, compiler_params=_params(3), name=name,
    )(*operands)
    return outs if len(outs) > 1 else outs[0]


def _prep_body(xp_ref, xs_ref, g_ref, x_ref, xb_ref, ss_ref, *, n_prompt_tiles):
    i = pl.program_id(0)

    def emit(x):
        x_ref[...] = x
        xb_ref[...] = (x * g_ref[...]).astype(BF16)
        ss_ref[...] = jnp.sum(x * x, axis=1, keepdims=True)

    @pl.when(i < n_prompt_tiles)
    def _():
        emit(xp_ref[...])

    @pl.when(i >= n_prompt_tiles)
    def _():
        emit(xs_ref[...])


def _prep(xp2, xs2, gain, gain_layer):
    mp, d = xp2.shape
    ms = xs2.shape[0]
    tr = _pick(ms, (256, 128, 64, 32, 16, 8))
    npt, nst = mp // tr, ms // tr
    m = mp + ms
    g3 = gain.reshape(-1, 1, d)
    row = pl.BlockSpec((tr, d), lambda i: (i, 0))
    return pl.pallas_call(
        functools.partial(_prep_body, n_prompt_tiles=npt),
        out_shape=[jax.ShapeDtypeStruct((m, d), F32), jax.ShapeDtypeStruct((m, d), BF16),
                   jax.ShapeDtypeStruct((m, 1), F32)],
        grid=(npt + nst,),
        in_specs=[pl.BlockSpec((tr, d), lambda i: (jnp.minimum(i, npt - 1), 0)),
                  pl.BlockSpec((tr, d), lambda i: (jnp.maximum(i - npt, 0), 0)),
                  pl.BlockSpec((None, 1, d), lambda i: (gain_layer, 0, 0))],
        out_specs=[row, row, pl.BlockSpec((tr, 1), lambda i: (i, 0))],
        compiler_params=_params(1), name="prep",
    )(xp2, xs2, g3)


def _final_body(x_ref, ss_ref, g_ref, y_ref, *, inv_d):
    y_ref[...] = x_ref[...] * lax.rsqrt(ss_ref[...] * inv_d + EPS) * g_ref[...]


def _final_norm(x, ss, gain, row0, rows, name):
    d = x.shape[1]
    tr = _pick(rows, (256, 128, 64, 32, 16, 8))
    off = row0 // tr
    return pl.pallas_call(
        functools.partial(_final_body, inv_d=1.0 / d),
        out_shape=jax.ShapeDtypeStruct((rows, d), F32),
        grid=(rows // tr,),
        in_specs=[pl.BlockSpec((tr, d), lambda i: (off + i, 0)),
                  pl.BlockSpec((tr, 1), lambda i: (off + i, 0)),
                  pl.BlockSpec((1, d), lambda i: (0, 0))],
        out_specs=pl.BlockSpec((tr, d), lambda i: (i, 0)),
        compiler_params=_params(1), name=name,
    )(x, ss, gain.reshape(1, d))


def _log_sigmoid(x):
    return jnp.minimum(x, 0.0) - jnp.log1p(jnp.exp(-jnp.abs(x)))


def _split3(x):
    hi = x.astype(BF16)
    r1 = x - hi.astype(F32)
    mid = r1.astype(BF16)
    lo = (r1 - mid.astype(F32)).astype(BF16)
    return hi, mid, lo


def _gla_log_decay(g1, wg2, bg):
    return _log_sigmoid(_dot(g1.astype(BF16), wg2.astype(BF16)) + bg) * (1.0 / GLA_TAU)


def _gla_chunk(q, k, v, g, s, *, sub, scale):
    c, hk = q.shape

    row = lax.broadcasted_iota(jnp.int32, (c, c), 0)
    col = lax.broadcasted_iota(jnp.int32, (c, c), 1)
    causal = col <= row
    tril = jnp.where(causal, 1.0, 0.0).astype(BF16)
    g_parts = _split3(g)
    b = _dot(tril, g_parts[0]) + _dot(tril, g_parts[1]) + _dot(tril, g_parts[2])
    ones = jnp.ones((c, LANES), BF16)
    b_last_col = (_dot_tn(g_parts[0], ones) + _dot_tn(g_parts[1], ones)
                  + _dot_tn(g_parts[2], ones))[:, :1]
    b_last = b[c - 1:c, :]

    qs = q * scale
    vb = v.astype(BF16)
    o = _dot((qs * jnp.exp(b)).astype(BF16), s.astype(BF16))

    key_pos = lax.broadcasted_iota(jnp.int32, (c, 1), 0)
    att_rows = []
    for i in range(c // sub):
        lo, hi = i * sub, (i + 1) * sub
        b_ref = b[lo - 1:lo, :] if i > 0 else jnp.zeros((1, hk), F32)
        qi = (qs[lo:hi] * jnp.exp(b[lo:hi] - b_ref)).astype(BF16)
        expo = jnp.where(key_pos < hi, b_ref - b, -1e30)
        kk = (k * jnp.exp(expo)).astype(BF16)
        att_rows.append(_dot_nt(qi, kk))
    att = att_rows[0] if len(att_rows) == 1 else jnp.concatenate(att_rows, axis=0)
    att = jnp.where(causal, att, 0.0)
    o = o + _dot(att.astype(BF16), vb)

    kd = (k * jnp.exp(b_last - b)).astype(BF16)
    s_new = jnp.exp(b_last_col) * s + _dot_tn(kd, vb)
    return o, s_new


def _gla_gate_out(o, rg, ng):
    on = o * lax.rsqrt(jnp.mean(o * o, axis=-1, keepdims=True) + EPS) * ng
    return (on * (rg * _sigmoid(rg))).astype(BF16)


def _gla_prompt_body(q_ref, k_ref, v_ref, g1_ref, wg2_ref, bg_ref, rg_ref, ng_ref,
                     og_ref, sout_ref, s_ref, *, nc, scale):
    c = pl.program_id(2)

    @pl.when(c == 0)
    def _():
        s_ref[...] = jnp.zeros_like(s_ref)

    g = _gla_log_decay(g1_ref[...], wg2_ref[...], bg_ref[...])
    o, s_new = _gla_chunk(q_ref[...], k_ref[...], v_ref[...], g, s_ref[...], sub=GLA_SUB, scale=scale)
    s_ref[...] = s_new
    og_ref[...] = _gla_gate_out(o, rg_ref[...], ng_ref[...])

    @pl.when(c == nc - 1)
    def _():
        sout_ref[...] = s_new


def _gla_sample_body(q_ref, k_ref, v_ref, g1_ref, wg2_ref, bg_ref, rg_ref, ng_ref, s0_ref, *rest,
                     scale, sub, nh):
    og_ref, sout_ref = rest[-2], rest[-1]
    hk, hv = s0_ref.shape[1], s0_ref.shape[2]
    g = _gla_log_decay(g1_ref[...], wg2_ref[...], bg_ref[...])
    for h in range(nh):
        ck = slice(h * hk, (h + 1) * hk)
        cv = slice(h * hv, (h + 1) * hv)
        o, s_new = _gla_chunk(q_ref[:, ck], k_ref[:, ck], v_ref[:, cv], g[:, ck], s0_ref[h],
                              sub=sub, scale=scale)
        og_ref[:, cv] = _gla_gate_out(o, rg_ref[:, cv], ng_ref[:, cv])
        sout_ref[h] = s_new


def _gla_scan(q, k, v, g1, rg, wg2p, bg, ng, state_in, layer_j, sp_prev, gs_prev, *,
              n_prompt, l_prompt, n_sample, l_sample):
    n_gla, _, nh, hk, hv = state_in.shape
    m = q.shape[0]
    scale = float(hk) ** -0.5
    rk = g1.shape[1]
    wg3 = wg2p
    bg3 = bg.reshape(n_gla, 1, nh * hk)
    ng3 = ng.reshape(n_gla, 1, nh * hv)

    cc = _pick(l_prompt, (GLA_CHUNK, 32, 16, 8))
    nc = l_prompt // cc
    sp_shape = jax.ShapeDtypeStruct((n_gla, n_prompt, nh, hk, hv), F32)
    rowsel = lambda b, h, c: (b * nc + c, h)
    in_specs = [pl.BlockSpec((cc, hk), rowsel), pl.BlockSpec((cc, hk), rowsel),
                pl.BlockSpec((cc, hv), rowsel),
                pl.BlockSpec((cc, rk), lambda b, h, c: (b * nc + c, 0)),
                pl.BlockSpec((None, rk, hk), lambda b, h, c: (layer_j, 0, h)),
                pl.BlockSpec((None, 1, hk), lambda b, h, c: (layer_j, 0, h)),
                pl.BlockSpec((cc, hv), rowsel),
                pl.BlockSpec((None, 1, hv), lambda b, h, c: (layer_j, 0, h))]
    operands = [q, k, v, g1, wg3, bg3, rg, ng3]
    aliases = {}
    if sp_prev is not None:
        in_specs.append(pl.BlockSpec(memory_space=pl.ANY))
        operands.append(sp_prev)
        aliases = {len(operands) - 1: 1}

    def prompt_body(*refs):
        if sp_prev is not None:
            refs = refs[:8] + refs[9:]
        _gla_prompt_body(*refs, nc=nc, scale=scale)

    og, sp = pl.pallas_call(
        prompt_body,
        out_shape=[jax.ShapeDtypeStruct((m, nh * hv), BF16), sp_shape],
        grid=(n_prompt, nh, nc), in_specs=in_specs,
        out_specs=[pl.BlockSpec((cc, hv), rowsel),
                   pl.BlockSpec((None, None, None, hk, hv), lambda b, h, c: (layer_j, b, h, 0, 0))],
        scratch_shapes=[pltpu.VMEM((hk, hv), F32)],
        input_output_aliases=aliases, compiler_params=_params(3), name="gla_prompt",
    )(*operands)

    r0 = (n_prompt * l_prompt) // l_sample
    rows_s = lambda b: (r0 + b, 0)
    wsel = lambda b: (layer_j, 0, 0)
    state_spec = pl.BlockSpec((None, None, nh, hk, hv), lambda b: (layer_j, b, 0, 0, 0))
    in_specs = [pl.BlockSpec((l_sample, nh * hk), rows_s), pl.BlockSpec((l_sample, nh * hk), rows_s),
                pl.BlockSpec((l_sample, nh * hv), rows_s),
                pl.BlockSpec((l_sample, rk), rows_s),
                pl.BlockSpec((None, rk, nh * hk), wsel),
                pl.BlockSpec((None, 1, nh * hk), wsel),
                pl.BlockSpec((l_sample, nh * hv), rows_s),
                pl.BlockSpec((None, 1, nh * hv), wsel),
                state_spec,
                pl.BlockSpec(memory_space=pl.ANY)]
    operands = [q, k, v, g1, wg3, bg3, rg, ng3, state_in, og]
    aliases = {9: 0}
    if gs_prev is not None:
        in_specs.append(pl.BlockSpec(memory_space=pl.ANY))
        operands.append(gs_prev)
        aliases[10] = 1
    og, gs_new = pl.pallas_call(
        functools.partial(_gla_sample_body, scale=scale, sub=min(GLA_SUB, l_sample), nh=nh),
        out_shape=[jax.ShapeDtypeStruct((m, nh * hv), BF16),
                   jax.ShapeDtypeStruct((n_gla, n_sample, nh, hk, hv), F32)],
        grid=(n_sample,), in_specs=in_specs,
        out_specs=[pl.BlockSpec((l_sample, nh * hv), rows_s), state_spec],
        input_output_aliases=aliases, compiler_params=_params(1), name="gla_sample",
    )(*operands)
    return og, sp, gs_new


def _softmax_rows(s):
    e = jnp.exp(s - jnp.max(s, axis=-1, keepdims=True))
    return e / jnp.sum(e, axis=-1, keepdims=True)


def _attn_prompt_body(q_ref, k_ref, v_ref, o_ref, *, scale):
    q = q_ref[...].astype(BF16)
    p = _softmax_rows(_dot_nt(q, k_ref[...].astype(BF16)) * scale)
    o_ref[...] = _dot(p.astype(BF16), v_ref[...].astype(BF16)).astype(BF16)


def _attn_sample_body(q_ref, k_ref, v_ref, o_in_ref, o_ref, *, scale, bb, ls):
    del o_in_ref
    nh, hd = k_ref.shape[2], k_ref.shape[3]
    kh = pltpu.einshape("bmhd->hbmd", k_ref[...])
    vh = pltpu.einshape("bmhd->hbmd", v_ref[...])
    for h in range(nh):
        cols = slice(h * hd, (h + 1) * hd)
        q = q_ref[:, cols].reshape(bb, ls, hd).astype(BF16)
        s = lax.dot_general(q, kh[h].astype(BF16), (((2,), (2,)), ((0,), (0,))),
                            preferred_element_type=F32) * scale
        p = _softmax_rows(s).astype(BF16)
        o = lax.dot_general(p, vh[h].astype(BF16), (((2,), (1,)), ((0,), (0,))),
                            preferred_element_type=F32)
        o_ref[:, cols] = o.reshape(bb * ls, hd).astype(BF16)


def _mem_attention(q, kp, vp, cache_k, cache_v, layer, *, n_prompt, l_prompt, n_sample, l_sample):
    m, dm = q.shape
    _, _, mem_len, nh, hd = cache_k.shape
    scale = float(hd) ** -0.5
    tq = _pick(l_prompt, (ATTN_ROWS,) + ROW_TILES)
    nt = l_prompt // tq
    o = pl.pallas_call(
        functools.partial(_attn_prompt_body, scale=scale),
        out_shape=jax.ShapeDtypeStruct((m, dm), BF16),
        grid=(n_prompt, nt, nh),
        in_specs=[pl.BlockSpec((tq, hd), lambda b, t, h: (b * nt + t, h)),
                  pl.BlockSpec((mem_len, hd), lambda b, t, h: (b, h)),
                  pl.BlockSpec((mem_len, hd), lambda b, t, h: (b, h))],
        out_specs=pl.BlockSpec((tq, hd), lambda b, t, h: (b * nt + t, h)),
        compiler_params=_params(3), name="attn_prompt",
    )(q, kp, vp)

    bb = _pick(n_sample, (8, 4, 2, 1))
    r0 = (n_prompt * l_prompt) // (bb * l_sample)
    cache_spec = pl.BlockSpec((None, bb, mem_len, nh, hd), lambda i: (layer, i, 0, 0, 0))
    o = pl.pallas_call(
        functools.partial(_attn_sample_body, scale=scale, bb=bb, ls=l_sample),
        out_shape=jax.ShapeDtypeStruct((m, dm), BF16),
        grid=(n_sample // bb,),
        in_specs=[pl.BlockSpec((bb * l_sample, dm), lambda i: (r0 + i, 0)),
                  cache_spec, cache_spec,
                  pl.BlockSpec(memory_space=pl.ANY)],
        out_specs=pl.BlockSpec((bb * l_sample, dm), lambda i: (r0 + i, 0)),
        input_output_aliases={3: 0}, compiler_params=_params(1), name="attn_sample",
    )(q, cache_k, cache_v, o)
    return o


def _shift_rows(u, hist_rows, t, axis):
    n = len(hist_rows)
    out = []
    for s in range(1, n + 1):
        shifted = pltpu.roll(u, s, axis=axis)
        for tt in range(s):
            shifted = jnp.where(t == tt, hist_rows[n - s + tt], shifted)
        out.append(shifted)
    return out


def _sconv_prompt_body(u_ref, prev_ref, bg_ref, w_ref, z_ref, *, tiles_per_seq):
    i = pl.program_id(0)
    u = u_ref[...]
    keep = jnp.where(i % tiles_per_seq == 0, 0.0, 1.0)
    prev = prev_ref[...] * keep
    kw = w_ref.shape[0]
    hist = [prev[8 - (kw - 1) + r: 8 - (kw - 1) + r + 1, :] for r in range(kw - 1)]
    t = lax.broadcasted_iota(jnp.int32, (u.shape[0], 1), 0)
    shifted = _shift_rows(u, hist, t, 0)
    y = w_ref[kw - 1:kw, :] * u
    for s, us in enumerate(shifted, start=1):
        y = y + w_ref[kw - 1 - s:kw - s, :] * us
    z_ref[...] = (bg_ref[...] * y).astype(BF16)


def _sconv_sample_body(u_ref, hist_ref, bg_ref, w_ref, z_in_ref, z_ref, *, bb, ls):
    del z_in_ref
    tc = u_ref.shape[1]
    u = u_ref[...].reshape(bb, ls, tc)
    kw = w_ref.shape[0]
    hist = [hist_ref[:, r:r + 1, :] for r in range(kw - 1)]
    t = lax.broadcasted_iota(jnp.int32, (1, ls, 1), 1)
    shifted = _shift_rows(u, hist, t, 1)
    y = w_ref[kw - 1:kw, :] * u
    for s, us in enumerate(shifted, start=1):
        y = y + w_ref[kw - 1 - s:kw - s, :] * us
    z_ref[...] = (bg_ref[...] * y.reshape(bb * ls, tc)).astype(BF16)


def _short_conv(u, bgate, wconv, layer_j, state, *, n_prompt, l_prompt, n_sample, l_sample):
    m, d = u.shape
    kw = wconv.shape[1]
    tr = _pick(l_prompt, (SCONV_ROWS,) + ROW_TILES)
    tc = _pick(d, (512, 256, 128))
    tps = l_prompt // tr
    z = pl.pallas_call(
        functools.partial(_sconv_prompt_body, tiles_per_seq=tps),
        out_shape=jax.ShapeDtypeStruct((m, d), BF16),
        grid=(n_prompt * tps, d // tc),
        in_specs=[pl.BlockSpec((tr, tc), lambda i, j: (i, j)),
                  pl.BlockSpec((8, tc), lambda i, j: (jnp.maximum(i * (tr // 8) - 1, 0), j)),
                  pl.BlockSpec((tr, tc), lambda i, j: (i, j)),
                  pl.BlockSpec((None, kw, tc), lambda i, j: (layer_j, 0, j))],
        out_specs=pl.BlockSpec((tr, tc), lambda i, j: (i, j)),
        compiler_params=_params(2), name="sconv_prompt",
    )(u, u, bgate, wconv)

    bb = _pick(n_sample, (64, 32, 16, 8, 4, 2, 1))
    r0 = (n_prompt * l_prompt) // (bb * l_sample)
    z = pl.pallas_call(
        functools.partial(_sconv_sample_body, bb=bb, ls=l_sample),
        out_shape=jax.ShapeDtypeStruct((m, d), BF16),
        grid=(n_sample // bb, d // tc),
        in_specs=[pl.BlockSpec((bb * l_sample, tc), lambda i, j: (r0 + i, j)),
                  pl.BlockSpec((None, bb, kw - 1, tc), lambda i, j: (layer_j, i, 0, j)),
                  pl.BlockSpec((bb * l_sample, tc), lambda i, j: (r0 + i, j)),
                  pl.BlockSpec((None, kw, tc), lambda i, j: (layer_j, 0, j)),
                  pl.BlockSpec(memory_space=pl.ANY)],
        out_specs=pl.BlockSpec((bb * l_sample, tc), lambda i, j: (r0 + i, j)),
        input_output_aliases={4: 0}, compiler_params=_params(2), name="sconv_sample",
    )(u, state, bgate, wconv, z)
    return z


def _cconv_core(u3, hist3, w_ref, bdw_ref, lg_ref, lb_ref, ext_ref, shift_ref, z_ref, out_ref, *, tc):
    j = pl.program_id(1)
    nj = pl.num_programs(1)
    bb, ll, _ = u3.shape
    hb = hist3.shape[1]
    kw = w_ref.shape[0]
    off = (-hb) % 8
    ext_ref[:, off:off + hb, :] = hist3
    ext_ref[:, off + hb:off + hb + ll, :] = u3
    base = off + hb - (kw - 1)
    acc = jnp.zeros((bb, ll, tc), F32) + bdw_ref[...]
    for r in range(min(8, kw)):
        n_a = (kw - 1 - r) // 8 + 1
        rows = ll + 8 * (n_a - 1)
        shift_ref[:, 0:rows, :] = ext_ref[:, base + r:base + r + rows, :]
        for a in range(n_a):
            kk = 8 * a + r
            acc = acc + w_ref[kk:kk + 1, :] * shift_ref[:, 8 * a:8 * a + ll, :]
    z_ref[j] = acc

    @pl.when(j == nj - 1)
    def _():
        n_tiles = z_ref.shape[0]
        inv_d = 1.0 / (n_tiles * tc)
        z = z_ref[...]
        mu = jnp.sum(jnp.sum(z, axis=3, keepdims=True), axis=0, keepdims=True) * inv_d
        zc = z - mu
        var = jnp.sum(jnp.sum(zc * zc, axis=3, keepdims=True), axis=0, keepdims=True) * inv_d
        zn = zc * lax.rsqrt(var + EPS)
        for jj in range(n_tiles):
            cols = slice(jj * tc, (jj + 1) * tc)
            y = zn[jj] * lg_ref[:, cols] + lb_ref[:, cols]
            y = y * _sigmoid(y)
            out_ref[:, cols] = y.reshape(bb * ll, tc).astype(BF16)


def _cconv_prompt_body(u_ref, halo_ref, w_ref, bdw_ref, lg_ref, lb_ref, out_ref, ext_ref, shift_ref,
                       z_ref, *, tiles_per_seq, tc):
    i = pl.program_id(0)
    keep = jnp.where(i % tiles_per_seq == 0, 0.0, 1.0)
    u3 = u_ref[...][None]
    hist3 = (halo_ref[...] * keep)[None]
    _cconv_core(u3, hist3, w_ref, bdw_ref, lg_ref, lb_ref, ext_ref, shift_ref, z_ref, out_ref, tc=tc)


def _cconv_sample_body(u_ref, hist_ref, w_ref, bdw_ref, lg_ref, lb_ref, o_in_ref, out_ref, ext_ref,
                       shift_ref, z_ref, *, bb, ls, tc):
    del o_in_ref
    u3 = u_ref[...].reshape(bb, ls, tc)
    _cconv_core(u3, hist_ref[...], w_ref, bdw_ref, lg_ref, lb_ref, ext_ref, shift_ref, z_ref, out_ref,
                tc=tc)


def _conformer_conv(u, wdw, bdw, lng, lnb, layer_j, state, *, n_prompt, l_prompt, n_sample, l_sample):
    m, d = u.shape
    kw = wdw.shape[1]
    halo = 8 * ((kw - 1 + 7) // 8)
    shift_pad = 8 * ((kw - 1) // 8)
    tr = _pick(l_prompt, (CCONV_ROWS, 64, 32))
    tc = _pick(d, (512, 256, 128))
    tps = l_prompt // tr
    b3 = bdw.reshape(-1, 1, d)
    g3 = lng.reshape(-1, 1, d)
    l3 = lnb.reshape(-1, 1, d)
    vec = lambda i, j: (layer_j, 0, j)
    full = lambda i, j: (layer_j, 0, 0)
    zc = pl.pallas_call(
        functools.partial(_cconv_prompt_body, tiles_per_seq=tps, tc=tc),
        out_shape=jax.ShapeDtypeStruct((m, d), BF16),
        grid=(n_prompt * tps, d // tc),
        in_specs=[pl.BlockSpec((tr, tc), lambda i, j: (i, j)),
                  pl.BlockSpec((halo, tc), lambda i, j: (jnp.maximum(i * (tr // halo) - 1, 0), j)),
                  pl.BlockSpec((None, kw, tc), vec), pl.BlockSpec((None, 1, tc), vec),
                  pl.BlockSpec((None, 1, d), full), pl.BlockSpec((None, 1, d), full)],
        out_specs=pl.BlockSpec((tr, d), lambda i, j: (i, 0)),
        scratch_shapes=[pltpu.VMEM((1, tr + halo, tc), F32), pltpu.VMEM((1, tr + shift_pad, tc), F32),
                        pltpu.VMEM((d // tc, 1, tr, tc), F32)],
        compiler_params=_params(2), name="cconv_prompt",
    )(u, u, wdw, b3, g3, l3)

    bb = _pick(n_sample, (8, 4, 2, 1))
    r0 = (n_prompt * l_prompt) // (bb * l_sample)
    hb = kw - 1
    ext_rows = 8 * (((-hb) % 8 + hb + l_sample + 7) // 8)
    zc = pl.pallas_call(
        functools.partial(_cconv_sample_body, bb=bb, ls=l_sample, tc=tc),
        out_shape=jax.ShapeDtypeStruct((m, d), BF16),
        grid=(n_sample // bb, d // tc),
        in_specs=[pl.BlockSpec((bb * l_sample, tc), lambda i, j: (r0 + i, j)),
                  pl.BlockSpec((None, bb, hb, tc), lambda i, j: (layer_j, i, 0, j)),
                  pl.BlockSpec((None, kw, tc), vec), pl.BlockSpec((None, 1, tc), vec),
                  pl.BlockSpec((None, 1, d), full), pl.BlockSpec((None, 1, d), full),
                  pl.BlockSpec(memory_space=pl.ANY)],
        out_specs=pl.BlockSpec((bb * l_sample, d), lambda i, j: (r0 + i, 0)),
        scratch_shapes=[pltpu.VMEM((bb, ext_rows, tc), F32),
                        pltpu.VMEM((bb, l_sample + shift_pad, tc), F32),
                        pltpu.VMEM((d // tc, bb, l_sample, tc), F32)],
        input_output_aliases={6: 0}, compiler_params=_params(2), name="cconv_sample",
    )(u, state, wdw, b3, g3, l3, zc)
    return zc


def kernel(x_prompt, x_sample, state_gla, state_sconv, state_cconv, cache_mem_k, cache_mem_v, mem_prompt, norm_mix, norm_mem, norm_ffn, norm_final, gla_wq, gla_wk, gla_wv, gla_wg1, gla_wg2, gla_bg, gla_wr, gla_norm, gla_wo, sc_win, sc_wconv, sc_wout, cf_w1, cf_b1, cf_wdw, cf_bdw, cf_ln_g, cf_ln_b, cf_w2, cf_b2, mem_wq, mem_wk, mem_wv, mem_wo, ffn_up, ffn_down):
    n_prompt, l_prompt, d = x_prompt.shape
    n_sample, l_sample, _ = x_sample.shape
    depth = norm_mix.shape[0]
    n_mixers = 3
    mp, ms = n_prompt * l_prompt, n_sample * l_sample
    seq = dict(n_prompt=n_prompt, l_prompt=l_prompt, n_sample=n_sample, l_sample=l_sample)
    _, _, mem_len, mem_heads, mem_hd = cache_mem_k.shape
    mem_dim = mem_heads * mem_hd
    n_gla, _, gla_heads, hk, hv = state_gla.shape
    kw_sc = sc_wconv.shape[1]
    kw_cf = cf_wdw.shape[1]

    mem2 = mem_prompt.reshape(n_prompt * mem_len, d)
    kps = [_matmul(mem2, mem_wk, i, kind="plain", name="mem_k", tm=512) for i in range(depth)]
    vps = [_matmul(mem2, mem_wv, i, kind="plain", name="mem_v", tm=512) for i in range(depth)]

    rank = gla_wg1.shape[-1]
    rpad = LANES * ((rank + LANES - 1) // LANES)
    wg1p = jnp.pad(gla_wg1, ((0, 0), (0, 0), (0, rpad - rank)))
    wg2p = jnp.pad(gla_wg2, ((0, 0), (0, rpad - rank), (0, 0)))

    x, xb, ss = _prep(x_prompt.reshape(mp, d), x_sample.reshape(ms, d), norm_mix, 0)

    gla_p = gla_s = None
    sc_us, cf_us = [], []
    for i in range(depth):
        kind, j = i % n_mixers, i // n_mixers
        if kind == 0:
            q = _matmul(xb, gla_wq, j, kind="plain", name="gla_q", ss=ss)
            k = _matmul(xb, gla_wk, j, kind="plain", name="gla_k", ss=ss)
            v = _matmul(xb, gla_wv, j, kind="plain", name="gla_v", ss=ss)
            rg = _matmul(xb, gla_wr, j, kind="plain", name="gla_r", ss=ss)
            g1 = _matmul(xb, wg1p, j, kind="plain", name="gla_g1", ss=ss, tn=LANES)
            mix, gla_p, gla_s = _gla_scan(q, k, v, g1, rg, wg2p, gla_bg, gla_norm, state_gla, j,
                                          gla_p, gla_s, **seq)
            x, xb, ss = _matmul(mix, gla_wo, j, kind="delta", name="gla_o", res=x,
                                gain=norm_mem, gain_layer=i)
        elif kind == 1:
            bgate, u = _matmul(xb, sc_win, j, kind="sconv", name="sc_in", groups=3, ss=ss, tn=256)
            sc_us.append(u)
            z = _short_conv(u, bgate, sc_wconv, j, state_sconv, **seq)
            x, xb, ss = _matmul(z, sc_wout, j, kind="delta", name="sc_out", res=x,
                                gain=norm_mem, gain_layer=i)
        else:
            u = _matmul(xb, cf_w1, j, kind="glu", name="cf_in", groups=2, ss=ss, bias=cf_b1,
                        bias_layer=j, tn=256)
            cf_us.append(u)
            zc = _conformer_conv(u, cf_wdw, cf_bdw, cf_ln_g, cf_ln_b, j, state_cconv, **seq)
            x, xb, ss = _matmul(zc, cf_w2, j, kind="delta", name="cf_out", res=x, bias=cf_b2,
                                bias_layer=j, gain=norm_mem, gain_layer=i)

        q = _matmul(xb, mem_wq, i, kind="plain", name="mem_q", ss=ss)
        o = _mem_attention(q, kps[i], vps[i], cache_mem_k, cache_mem_v, i, **seq)
        x, xb, ss = _matmul(o, mem_wo, i, kind="delta", name="mem_o", res=x, gain=norm_ffn,
                            gain_layer=i, tn=1024)

        hid = _matmul(xb, ffn_up, i, kind="relu2", name="ffn_up", ss=ss, out_dtype=BF16)
        last = i == depth - 1
        x, xb, ss = _matmul(hid, ffn_down, i, kind="delta", name="ffn_down", res=x,
                            gain=norm_final.reshape(1, d) if last else norm_mix,
                            gain_layer=0 if last else i + 1, tm=FFN_DOWN_ROWS, tn=1024, tk=1024)

    y_prompt = _final_norm(x, ss, norm_final, 0, mp, "final_prompt").reshape(n_prompt, l_prompt, d)
    y_sample = _final_norm(x, ss, norm_final, mp, ms, "final_sample").reshape(n_sample, l_sample, d)

    def tails(us_, states, keep):
        tails_p, tails_s = [], []
        lead = max(l_prompt - keep, 0)
        for jj, u in enumerate(us_):
            up = jnp.stack([u[b * l_prompt + lead:(b + 1) * l_prompt] for b in range(n_prompt)])
            us = u[mp:].reshape(n_sample, l_sample, d)
            tails_p.append(jnp.concatenate([jnp.zeros((n_prompt, keep, d), F32), up], axis=1)[:, -keep:])
            tails_s.append(jnp.concatenate([states[jj], us], axis=1)[:, -keep:])
        return jnp.stack(tails_p), jnp.stack(tails_s)

    sc_p, sc_s = tails(sc_us, state_sconv, kw_sc - 1)
    cf_p, cf_s = tails(cf_us, state_cconv, kw_cf - 1)
    mem_k_p = jnp.stack(kps).reshape(depth, n_prompt, mem_len, mem_heads, mem_hd)
    mem_v_p = jnp.stack(vps).reshape(depth, n_prompt, mem_len, mem_heads, mem_hd)
    return (y_prompt, y_sample, gla_p, gla_s, sc_p, sc_s, cf_p, cf_s, mem_k_p, mem_v_p)
```

```python
import functools

import jax
import jax.numpy as jnp
from jax import lax
from jax.experimental import pallas as pl
from jax.experimental.pallas import tpu as pltpu

F32 = jnp.float32
BF16 = jnp.bfloat16

EPS = 1e-6
GLA_TAU = 16.0
GLA_SUB = 16
GLA_CHUNK = 128
GLA_PROMPT_HEADS = 2
V7X_VMEM_LIMIT_BYTES = 56 * 1024 * 1024
LANES = 128
MM_ROWS = 1024
WIDE_ROWS = 1536
FFN_DOWN_ROWS = 1536
ROW_TILES = (512, 256, 128, 64, 32, 16, 8)
SCONV_ROWS = 512
CCONV_ROWS = 128
ATTN_ROWS = 512


def _params(n_axes):
    return pltpu.CompilerParams(
        dimension_semantics=("arbitrary",) * n_axes,
        vmem_limit_bytes=V7X_VMEM_LIMIT_BYTES,
    )


def _dot(a, b):
    return jnp.dot(a, b, preferred_element_type=F32)


def _dot_nt(a, b):
    return lax.dot_general(a, b, (((1,), (1,)), ((), ())), preferred_element_type=F32)


def _dot_tn(a, b):
    return lax.dot_general(a, b, (((0,), (0,)), ((), ())), preferred_element_type=F32)


def _sigmoid(x):
    return 1.0 / (1.0 + jnp.exp(-x))


def _pick(n, prefs):
    for p in prefs:
        if p <= n and n % p == 0:
            return p
    return n


def _mm_body(*refs, groups, nk, kind, has_ss, has_bias, inv_d):
    it = iter(refs)
    a_ref = next(it)
    w_refs = [next(it) for _ in range(groups)]
    ss_ref = next(it) if has_ss else None
    b_refs = [next(it) for _ in range(groups)] if has_bias else None
    if kind == "delta":
        res_ref = next(it)
        gain_ref = next(it)
    n_out = {"plain": 1, "relu2": 1, "glu": 1, "sconv": 2, "delta": 3}[kind]
    out_refs = [next(it) for _ in range(n_out)]
    acc_refs = [next(it) for _ in range(groups)] if nk > 1 and kind != "delta" else None

    j = pl.program_id(1)
    k = pl.program_id(2)

    def products():
        a = a_ref[...]
        if a.dtype != BF16:
            a = a.astype(BF16)
        return [_dot(a, w_ref[...].astype(BF16)) for w_ref in w_refs]

    def epilogue(accs):
        if has_ss:
            r = lax.rsqrt(ss_ref[...] * inv_d + EPS)
            accs = [x * r for x in accs]
        if has_bias:
            accs = [x + b_ref[...] for x, b_ref in zip(accs, b_refs)]
        if kind == "plain":
            out_refs[0][...] = accs[0].astype(out_refs[0].dtype)
        elif kind == "relu2":
            h = jnp.maximum(accs[0], 0.0)
            out_refs[0][...] = (h * h).astype(out_refs[0].dtype)
        elif kind == "glu":
            out_refs[0][...] = accs[0] * _sigmoid(accs[1])
        else:
            assert kind == "sconv"
            out_refs[0][...] = accs[0]
            out_refs[1][...] = accs[1] * accs[2]

    def delta_tail(xn):
        out_refs[1][...] = (xn * gain_ref[...]).astype(BF16)
        s = jnp.sum(xn * xn, axis=1, keepdims=True)

        @pl.when(j == 0)
        def _():
            out_refs[2][...] = s

        @pl.when(j > 0)
        def _():
            out_refs[2][...] += s

    if nk == 1 and kind == "delta":
        xn = products()[0]
        if has_bias:
            xn = xn + b_refs[0][...]
        xn = res_ref[...] + xn
        out_refs[0][...] = xn
        delta_tail(xn)
    elif nk == 1:
        epilogue(products())
    elif kind == "delta":
        @pl.when(k == 0)
        def _():
            seed = res_ref[...]
            if has_bias:
                seed = seed + b_refs[0][...]
            out_refs[0][...] = seed

        out_refs[0][...] = out_refs[0][...] + products()[0]

        @pl.when(k == nk - 1)
        def _():
            delta_tail(out_refs[0][...])
    else:
        @pl.when(k == 0)
        def _():
            for acc_ref in acc_refs:
                acc_ref[...] = jnp.zeros_like(acc_ref)

        for acc_ref, p in zip(acc_refs, products()):
            acc_ref[...] = acc_ref[...] + p

        @pl.when(k == nk - 1)
        def _():
            epilogue([acc_ref[...] for acc_ref in acc_refs])


def _matmul(a, w, layer, *, kind, name, groups=1, ss=None, bias=None, bias_layer=0,
            res=None, gain=None, gain_layer=0, out_dtype=F32, tm=None, tn=512, tk=None):
    m, kdim = a.shape
    n = w.shape[-1] // groups
    tm = _pick(m, (MM_ROWS if tm is None else tm,) + ROW_TILES)
    tn = _pick(n, (tn, 512, 256, 128))
    tk = kdim if tk is None else _pick(kdim, (tk, 1024, 512, 256, 128))
    nk = kdim // tk
    nj = n // tn
    grid = (m // tm, nj, nk)
    has_ss = ss is not None
    has_bias = bias is not None

    in_specs = [pl.BlockSpec((tm, tk), lambda i, j, k: (i, k))]
    operands = [a]
    for g in range(groups):
        in_specs.append(pl.BlockSpec((None, tk, tn), lambda i, j, k, g=g: (layer, k, g * nj + j)))
        operands.append(w)
    if has_ss:
        in_specs.append(pl.BlockSpec((tm, 1), lambda i, j, k: (i, 0)))
        operands.append(ss)
    if has_bias:
        b3 = bias.reshape(bias.shape[0], 1, bias.shape[-1])
        for g in range(groups):
            in_specs.append(pl.BlockSpec((None, 1, tn), lambda i, j, k, g=g: (bias_layer, 0, g * nj + j)))
            operands.append(b3)
    tile = pl.BlockSpec((tm, tn), lambda i, j, k: (i, j))
    if kind == "delta":
        g3 = gain.reshape(-1, 1, gain.shape[-1])
        in_specs += [tile, pl.BlockSpec((None, 1, tn), lambda i, j, k: (gain_layer, 0, j))]
        operands += [res, g3]
        out_shape = [jax.ShapeDtypeStruct((m, n), F32), jax.ShapeDtypeStruct((m, n), BF16),
                     jax.ShapeDtypeStruct((m, 1), F32)]
        out_specs = [tile, tile, pl.BlockSpec((tm, 1), lambda i, j, k: (i, 0))]
    elif kind == "sconv":
        out_shape = [jax.ShapeDtypeStruct((m, n), F32)] * 2
        out_specs = [tile, tile]
    else:
        out_shape = [jax.ShapeDtypeStruct((m, n), out_dtype)]
        out_specs = [tile]
    scratch = [pltpu.VMEM((tm, tn), F32) for _ in range(groups)] if nk > 1 and kind != "delta" else []

    body = functools.partial(_mm_body, groups=groups, nk=nk, kind=kind, has_ss=has_ss,
                             has_bias=has_bias, inv_d=1.0 / kdim)
    outs = pl.pallas_call(
        body, out_shape=out_shape, grid=grid, in_specs=in_specs, out_specs=out_specs,
        scratch_shapes=scratch, compiler_params=_params(3), name=name,
    )(*operands)
    return outs if len(outs) > 1 else outs[0]


def _prep_body(xp_ref, xs_ref, g_ref, x_ref, xb_ref, ss_ref, *, n_prompt_tiles):
    i = pl.program_id(0)

    def emit(x):
        x_ref[...] = x
        xb_ref[...] = (x * g_ref[...]).astype(BF16)
        ss_ref[...] = jnp.sum(x * x, axis=1, keepdims=True)

    @pl.when(i < n_prompt_tiles)
    def _():
        emit(xp_ref[...])

    @pl.when(i >= n_prompt_tiles)
    def _():
        emit(xs_ref[...])


def _prep(xp2, xs2, gain, gain_layer):
    mp, d = xp2.shape
    ms = xs2.shape[0]
    tr = _pick(ms, (256, 128, 64, 32, 16, 8))
    npt, nst = mp // tr, ms // tr
    m = mp + ms
    g3 = gain.reshape(-1, 1, d)
    row = pl.BlockSpec((tr, d), lambda i: (i, 0))
    return pl.pallas_call(
        functools.partial(_prep_body, n_prompt_tiles=npt),
        out_shape=[jax.ShapeDtypeStruct((m, d), F32), jax.ShapeDtypeStruct((m, d), BF16),
                   jax.ShapeDtypeStruct((m, 1), F32)],
        grid=(npt + nst,),
        in_specs=[pl.BlockSpec((tr, d), lambda i: (jnp.minimum(i, npt - 1), 0)),
                  pl.BlockSpec((tr, d), lambda i: (jnp.maximum(i - npt, 0), 0)),
                  pl.BlockSpec((None, 1, d), lambda i: (gain_layer, 0, 0))],
        out_specs=[row, row, pl.BlockSpec((tr, 1), lambda i: (i, 0))],
        compiler_params=_params(1), name="prep",
    )(xp2, xs2, g3)


def _final_body(x_ref, ss_ref, g_ref, y_ref, *, inv_d):
    y_ref[...] = x_ref[...] * lax.rsqrt(ss_ref[...] * inv_d + EPS) * g_ref[...]


def _final_norm(x, ss, gain, row0, rows, name):
    d = x.shape[1]
    tr = _pick(rows, (256, 128, 64, 32, 16, 8))
    off = row0 // tr
    return pl.pallas_call(
        functools.partial(_final_body, inv_d=1.0 / d),
        out_shape=jax.ShapeDtypeStruct((rows, d), F32),
        grid=(rows // tr,),
        in_specs=[pl.BlockSpec((tr, d), lambda i: (off + i, 0)),
                  pl.BlockSpec((tr, 1), lambda i: (off + i, 0)),
                  pl.BlockSpec((1, d), lambda i: (0, 0))],
        out_specs=pl.BlockSpec((tr, d), lambda i: (i, 0)),
        compiler_params=_params(1), name=name,
    )(x, ss, gain.reshape(1, d))


def _log_sigmoid(x):
    return jnp.minimum(x, 0.0) - jnp.log1p(jnp.exp(-jnp.abs(x)))


def _split3(x):
    hi = x.astype(BF16)
    r1 = x - hi.astype(F32)
    mid = r1.astype(BF16)
    lo = (r1 - mid.astype(F32)).astype(BF16)
    return hi, mid, lo


def _gla_log_decay(g1, wg2, bg):
    return _log_sigmoid(_dot(g1.astype(BF16), wg2.astype(BF16)) + bg) * (1.0 / GLA_TAU)


def _gla_chunk(q, k, v, g, s, *, sub, scale):
    c, hk = q.shape

    row = lax.broadcasted_iota(jnp.int32, (c, c), 0)
    col = lax.broadcasted_iota(jnp.int32, (c, c), 1)
    causal = col <= row
    tril = jnp.where(causal, 1.0, 0.0).astype(BF16)
    g_parts = _split3(g)
    b = _dot(tril, g_parts[0]) + _dot(tril, g_parts[1]) + _dot(tril, g_parts[2])
    ones = jnp.ones((c, LANES), BF16)
    b_last_col = (_dot_tn(g_parts[0], ones) + _dot_tn(g_parts[1], ones)
                  + _dot_tn(g_parts[2], ones))[:, :1]
    b_last = b[c - 1:c, :]

    qs = q * scale
    vb = v.astype(BF16)
    o = _dot((qs * jnp.exp(b)).astype(BF16), s.astype(BF16))

    key_pos = lax.broadcasted_iota(jnp.int32, (c, 1), 0)
    att_rows = []
    for i in range(c // sub):
        lo, hi = i * sub, (i + 1) * sub
        b_ref = b[lo - 1:lo, :] if i > 0 else jnp.zeros((1, hk), F32)
        qi = (qs[lo:hi] * jnp.exp(b[lo:hi] - b_ref)).astype(BF16)
        expo = jnp.where(key_pos < hi, b_ref - b, -1e30)
        kk = (k * jnp.exp(expo)).astype(BF16)
        att_rows.append(_dot_nt(qi, kk))
    att = att_rows[0] if len(att_rows) == 1 else jnp.concatenate(att_rows, axis=0)
    att = jnp.where(causal, att, 0.0)
    o = o + _dot(att.astype(BF16), vb)

    kd = (k * jnp.exp(b_last - b)).astype(BF16)
    s_new = jnp.exp(b_last_col) * s + _dot_tn(kd, vb)
    return o, s_new


def _gla_gate_out(o, rg, ng):
    on = o * lax.rsqrt(jnp.mean(o * o, axis=-1, keepdims=True) + EPS) * ng
    return (on * (rg * _sigmoid(rg))).astype(BF16)


def _gla_prompt_body(q_ref, k_ref, v_ref, g1_ref, wg2_ref, bg_ref, rg_ref, ng_ref,
                     og_ref, sout_ref, s_ref, *, nc, scale):
    c = pl.program_id(2)

    @pl.when(c == 0)
    def _():
        s_ref[...] = jnp.zeros_like(s_ref)

    g = _gla_log_decay(g1_ref[...], wg2_ref[...], bg_ref[...])
    hk, hv = s_ref.shape[1], s_ref.shape[2]
    for h in range(s_ref.shape[0]):
        ck = slice(h * hk, (h + 1) * hk)
        cv = slice(h * hv, (h + 1) * hv)
        o, s_new = _gla_chunk(q_ref[:, ck], k_ref[:, ck], v_ref[:, cv], g[:, ck], s_ref[h],
                              sub=GLA_SUB, scale=scale)
        s_ref[h] = s_new
        og_ref[:, cv] = _gla_gate_out(o, rg_ref[:, cv], ng_ref[:, cv])

    @pl.when(c == nc - 1)
    def _():
        sout_ref[...] = s_ref[...]


def _gla_sample_body(q_ref, k_ref, v_ref, g1_ref, wg2_ref, bg_ref, rg_ref, ng_ref, s0_ref, *rest,
                     scale, sub, nh):
    og_ref, sout_ref = rest[-2], rest[-1]
    hk, hv = s0_ref.shape[1], s0_ref.shape[2]
    g = _gla_log_decay(g1_ref[...], wg2_ref[...], bg_ref[...])
    for h in range(nh):
        ck = slice(h * hk, (h + 1) * hk)
        cv = slice(h * hv, (h + 1) * hv)
        o, s_new = _gla_chunk(q_ref[:, ck], k_ref[:, ck], v_ref[:, cv], g[:, ck], s0_ref[h],
                              sub=sub, scale=scale)
        og_ref[:, cv] = _gla_gate_out(o, rg_ref[:, cv], ng_ref[:, cv])
        sout_ref[h] = s_new


def _gla_scan(q, k, v, g1, rg, wg2p, bg, ng, state_in, layer_j, sp_prev, gs_prev, *,
              n_prompt, l_prompt, n_sample, l_sample):
    n_gla, _, nh, hk, hv = state_in.shape
    m = q.shape[0]
    scale = float(hk) ** -0.5
    rk = g1.shape[1]
    wg3 = wg2p
    bg3 = bg.reshape(n_gla, 1, nh * hk)
    ng3 = ng.reshape(n_gla, 1, nh * hv)

    cc = _pick(l_prompt, (GLA_CHUNK, 32, 16, 8))
    nc = l_prompt // cc
    sp_shape = jax.ShapeDtypeStruct((n_gla, n_prompt, nh, hk, hv), F32)
    hps = _pick(nh, (GLA_PROMPT_HEADS, 1))
    rowsel = lambda b, h, c: (b * nc + c, h)
    in_specs = [pl.BlockSpec((cc, hps * hk), rowsel), pl.BlockSpec((cc, hps * hk), rowsel),
                pl.BlockSpec((cc, hps * hv), rowsel),
                pl.BlockSpec((cc, rk), lambda b, h, c: (b * nc + c, 0)),
                pl.BlockSpec((None, rk, hps * hk), lambda b, h, c: (layer_j, 0, h)),
                pl.BlockSpec((None, 1, hps * hk), lambda b, h, c: (layer_j, 0, h)),
                pl.BlockSpec((cc, hps * hv), rowsel),
                pl.BlockSpec((None, 1, hps * hv), lambda b, h, c: (layer_j, 0, h))]
    operands = [q, k, v, g1, wg3, bg3, rg, ng3]
    aliases = {}
    if sp_prev is not None:
        in_specs.append(pl.BlockSpec(memory_space=pl.ANY))
        operands.append(sp_prev)
        aliases = {len(operands) - 1: 1}

    def prompt_body(*refs):
        if sp_prev is not None:
            refs = refs[:8] + refs[9:]
        _gla_prompt_body(*refs, nc=nc, scale=scale)

    og, sp = pl.pallas_call(
        prompt_body,
        out_shape=[jax.ShapeDtypeStruct((m, nh * hv), BF16), sp_shape],
        grid=(n_prompt, nh // hps, nc), in_specs=in_specs,
        out_specs=[pl.BlockSpec((cc, hps * hv), rowsel),
                   pl.BlockSpec((None, None, hps, hk, hv), lambda b, h, c: (layer_j, b, h, 0, 0))],
        scratch_shapes=[pltpu.VMEM((hps, hk, hv), F32)],
        input_output_aliases=aliases, compiler_params=_params(3), name="gla_prompt",
    )(*operands)

    r0 = (n_prompt * l_prompt) // l_sample
    rows_s = lambda b: (r0 + b, 0)
    wsel = lambda b: (layer_j, 0, 0)
    state_spec = pl.BlockSpec((None, None, nh, hk, hv), lambda b: (layer_j, b, 0, 0, 0))
    in_specs = [pl.BlockSpec((l_sample, nh * hk), rows_s), pl.BlockSpec((l_sample, nh * hk), rows_s),
                pl.BlockSpec((l_sample, nh * hv), rows_s),
                pl.BlockSpec((l_sample, rk), rows_s),
                pl.BlockSpec((None, rk, nh * hk), wsel),
                pl.BlockSpec((None, 1, nh * hk), wsel),
                pl.BlockSpec((l_sample, nh * hv), rows_s),
                pl.BlockSpec((None, 1, nh * hv), wsel),
                state_spec,
                pl.BlockSpec(memory_space=pl.ANY)]
    operands = [q, k, v, g1, wg3, bg3, rg, ng3, state_in, og]
    aliases = {9: 0}
    if gs_prev is not None:
        in_specs.append(pl.BlockSpec(memory_space=pl.ANY))
        operands.append(gs_prev)
        aliases[10] = 1
    og, gs_new = pl.pallas_call(
        functools.partial(_gla_sample_body, scale=scale, sub=min(GLA_SUB, l_sample), nh=nh),
        out_shape=[jax.ShapeDtypeStruct((m, nh * hv), BF16),
                   jax.ShapeDtypeStruct((n_gla, n_sample, nh, hk, hv), F32)],
        grid=(n_sample,), in_specs=in_specs,
        out_specs=[pl.BlockSpec((l_sample, nh * hv), rows_s), state_spec],
        input_output_aliases=aliases, compiler_params=_params(1), name="gla_sample",
    )(*operands)
    return og, sp, gs_new


def _softmax_rows(s):
    e = jnp.exp(s - jnp.max(s, axis=-1, keepdims=True))
    return e / jnp.sum(e, axis=-1, keepdims=True)


def _attn_prompt_body(q_ref, k_ref, v_ref, o_ref, *, scale, nh, hd):
    for h in range(nh):
        cols = slice(h * hd, (h + 1) * hd)
        q = q_ref[:, cols].astype(BF16)
        p = _softmax_rows(_dot_nt(q, k_ref[:, cols].astype(BF16)) * scale)
        o_ref[:, cols] = _dot(p.astype(BF16), v_ref[:, cols].astype(BF16)).astype(BF16)


def _attn_sample_body(q_ref, k_ref, v_ref, o_in_ref, o_ref, *, scale, bb, ls):
    del o_in_ref
    nh, hd = k_ref.shape[2], k_ref.shape[3]
    kh = pltpu.einshape("bmhd->hbmd", k_ref[...])
    vh = pltpu.einshape("bmhd->hbmd", v_ref[...])
    for h in range(nh):
        cols = slice(h * hd, (h + 1) * hd)
        q = q_ref[:, cols].reshape(bb, ls, hd).astype(BF16)
        s = lax.dot_general(q, kh[h].astype(BF16), (((2,), (2,)), ((0,), (0,))),
                            preferred_element_type=F32) * scale
        p = _softmax_rows(s).astype(BF16)
        o = lax.dot_general(p, vh[h].astype(BF16), (((2,), (1,)), ((0,), (0,))),
                            preferred_element_type=F32)
        o_ref[:, cols] = o.reshape(bb * ls, hd).astype(BF16)


def _mem_attention(q, kp, vp, cache_k, cache_v, layer, *, n_prompt, l_prompt, n_sample, l_sample):
    m, dm = q.shape
    _, _, mem_len, nh, hd = cache_k.shape
    scale = float(hd) ** -0.5
    tq = _pick(l_prompt, (ATTN_ROWS,) + ROW_TILES)
    nt = l_prompt // tq
    o = pl.pallas_call(
        functools.partial(_attn_prompt_body, scale=scale, nh=nh, hd=hd),
        out_shape=jax.ShapeDtypeStruct((m, dm), BF16),
        grid=(n_prompt, nt),
        in_specs=[pl.BlockSpec((tq, dm), lambda b, t: (b * nt + t, 0)),
                  pl.BlockSpec((mem_len, dm), lambda b, t: (b, 0)),
                  pl.BlockSpec((mem_len, dm), lambda b, t: (b, 0))],
        out_specs=pl.BlockSpec((tq, dm), lambda b, t: (b * nt + t, 0)),
        compiler_params=_params(2), name="attn_prompt",
    )(q, kp, vp)

    bb = _pick(n_sample, (8, 4, 2, 1))
    r0 = (n_prompt * l_prompt) // (bb * l_sample)
    cache_spec = pl.BlockSpec((None, bb, mem_len, nh, hd), lambda i: (layer, i, 0, 0, 0))
    o = pl.pallas_call(
        functools.partial(_attn_sample_body, scale=scale, bb=bb, ls=l_sample),
        out_shape=jax.ShapeDtypeStruct((m, dm), BF16),
        grid=(n_sample // bb,),
        in_specs=[pl.BlockSpec((bb * l_sample, dm), lambda i: (r0 + i, 0)),
                  cache_spec, cache_spec,
                  pl.BlockSpec(memory_space=pl.ANY)],
        out_specs=pl.BlockSpec((bb * l_sample, dm), lambda i: (r0 + i, 0)),
        input_output_aliases={3: 0}, compiler_params=_params(1), name="attn_sample",
    )(q, cache_k, cache_v, o)
    return o


def _shift_rows(u, hist_rows, t, axis):
    n = len(hist_rows)
    out = []
    for s in range(1, n + 1):
        shifted = pltpu.roll(u, s, axis=axis)
        for tt in range(s):
            shifted = jnp.where(t == tt, hist_rows[n - s + tt], shifted)
        out.append(shifted)
    return out


def _sconv_prompt_body(u_ref, prev_ref, bg_ref, w_ref, z_ref, *, tiles_per_seq):
    i = pl.program_id(0)
    u = u_ref[...]
    keep = jnp.where(i % tiles_per_seq == 0, 0.0, 1.0)
    prev = prev_ref[...] * keep
    kw = w_ref.shape[0]
    hist = [prev[8 - (kw - 1) + r: 8 - (kw - 1) + r + 1, :] for r in range(kw - 1)]
    t = lax.broadcasted_iota(jnp.int32, (u.shape[0], 1), 0)
    shifted = _shift_rows(u, hist, t, 0)
    y = w_ref[kw - 1:kw, :] * u
    for s, us in enumerate(shifted, start=1):
        y = y + w_ref[kw - 1 - s:kw - s, :] * us
    z_ref[...] = (bg_ref[...] * y).astype(BF16)


def _sconv_sample_body(u_ref, hist_ref, bg_ref, w_ref, z_in_ref, z_ref, *, bb, ls):
    del z_in_ref
    tc = u_ref.shape[1]
    u = u_ref[...].reshape(bb, ls, tc)
    kw = w_ref.shape[0]
    hist = [hist_ref[:, r:r + 1, :] for r in range(kw - 1)]
    t = lax.broadcasted_iota(jnp.int32, (1, ls, 1), 1)
    shifted = _shift_rows(u, hist, t, 1)
    y = w_ref[kw - 1:kw, :] * u
    for s, us in enumerate(shifted, start=1):
        y = y + w_ref[kw - 1 - s:kw - s, :] * us
    z_ref[...] = (bg_ref[...] * y.reshape(bb * ls, tc)).astype(BF16)


def _short_conv(u, bgate, wconv, layer_j, state, *, n_prompt, l_prompt, n_sample, l_sample):
    m, d = u.shape
    kw = wconv.shape[1]
    tr = _pick(l_prompt, (SCONV_ROWS,) + ROW_TILES)
    tc = _pick(d, (512, 256, 128))
    tps = l_prompt // tr
    z = pl.pallas_call(
        functools.partial(_sconv_prompt_body, tiles_per_seq=tps),
        out_shape=jax.ShapeDtypeStruct((m, d), BF16),
        grid=(n_prompt * tps, d // tc),
        in_specs=[pl.BlockSpec((tr, tc), lambda i, j: (i, j)),
                  pl.BlockSpec((8, tc), lambda i, j: (jnp.maximum(i * (tr // 8) - 1, 0), j)),
                  pl.BlockSpec((tr, tc), lambda i, j: (i, j)),
                  pl.BlockSpec((None, kw, tc), lambda i, j: (layer_j, 0, j))],
        out_specs=pl.BlockSpec((tr, tc), lambda i, j: (i, j)),
        compiler_params=_params(2), name="sconv_prompt",
    )(u, u, bgate, wconv)

    bb = _pick(n_sample, (64, 32, 16, 8, 4, 2, 1))
    r0 = (n_prompt * l_prompt) // (bb * l_sample)
    z = pl.pallas_call(
        functools.partial(_sconv_sample_body, bb=bb, ls=l_sample),
        out_shape=jax.ShapeDtypeStruct((m, d), BF16),
        grid=(n_sample // bb, d // tc),
        in_specs=[pl.BlockSpec((bb * l_sample, tc), lambda i, j: (r0 + i, j)),
                  pl.BlockSpec((None, bb, kw - 1, tc), lambda i, j: (layer_j, i, 0, j)),
                  pl.BlockSpec((bb * l_sample, tc), lambda i, j: (r0 + i, j)),
                  pl.BlockSpec((None, kw, tc), lambda i, j: (layer_j, 0, j)),
                  pl.BlockSpec(memory_space=pl.ANY)],
        out_specs=pl.BlockSpec((bb * l_sample, tc), lambda i, j: (r0 + i, j)),
        input_output_aliases={4: 0}, compiler_params=_params(2), name="sconv_sample",
    )(u, state, bgate, wconv, z)
    return z


def _cconv_core(u3, hist3, w_ref, bdw_ref, lg_ref, lb_ref, ext_ref, shift_ref, z_ref, out_ref, *, tc):
    j = pl.program_id(1)
    nj = pl.num_programs(1)
    bb, ll, _ = u3.shape
    hb = hist3.shape[1]
    kw = w_ref.shape[0]
    off = (-hb) % 8
    ext_ref[:, off:off + hb, :] = hist3
    ext_ref[:, off + hb:off + hb + ll, :] = u3
    base = off + hb - (kw - 1)
    acc = jnp.zeros((bb, ll, tc), F32) + bdw_ref[...]
    for r in range(min(8, kw)):
        n_a = (kw - 1 - r) // 8 + 1
        rows = ll + 8 * (n_a - 1)
        shift_ref[:, 0:rows, :] = ext_ref[:, base + r:base + r + rows, :]
        for a in range(n_a):
            kk = 8 * a + r
            acc = acc + w_ref[kk:kk + 1, :] * shift_ref[:, 8 * a:8 * a + ll, :]
    z_ref[j] = acc

    @pl.when(j == nj - 1)
    def _():
        n_tiles = z_ref.shape[0]
        inv_d = 1.0 / (n_tiles * tc)
        z = z_ref[...]
        mu = jnp.sum(jnp.sum(z, axis=3, keepdims=True), axis=0, keepdims=True) * inv_d
        zc = z - mu
        var = jnp.sum(jnp.sum(zc * zc, axis=3, keepdims=True), axis=0, keepdims=True) * inv_d
        zn = zc * lax.rsqrt(var + EPS)
        for jj in range(n_tiles):
            cols = slice(jj * tc, (jj + 1) * tc)
            y = zn[jj] * lg_ref[:, cols] + lb_ref[:, cols]
            y = y * _sigmoid(y)
            out_ref[:, cols] = y.reshape(bb * ll, tc).astype(BF16)


def _cconv_prompt_body(u_ref, halo_ref, w_ref, bdw_ref, lg_ref, lb_ref, out_ref, ext_ref, shift_ref,
                       z_ref, *, tiles_per_seq, tc):
    i = pl.program_id(0)
    keep = jnp.where(i % tiles_per_seq == 0, 0.0, 1.0)
    u3 = u_ref[...][None]
    hist3 = (halo_ref[...] * keep)[None]
    _cconv_core(u3, hist3, w_ref, bdw_ref, lg_ref, lb_ref, ext_ref, shift_ref, z_ref, out_ref, tc=tc)


def _cconv_sample_body(u_ref, hist_ref, w_ref, bdw_ref, lg_ref, lb_ref, o_in_ref, out_ref, ext_ref,
                       shift_ref, z_ref, *, bb, ls, tc):
    del o_in_ref
    u3 = u_ref[...].reshape(bb, ls, tc)
    _cconv_core(u3, hist_ref[...], w_ref, bdw_ref, lg_ref, lb_ref, ext_ref, shift_ref, z_ref, out_ref,
                tc=tc)


def _conformer_conv(u, wdw, bdw, lng, lnb, layer_j, state, *, n_prompt, l_prompt, n_sample, l_sample):
    m, d = u.shape
    kw = wdw.shape[1]
    halo = 8 * ((kw - 1 + 7) // 8)
    shift_pad = 8 * ((kw - 1) // 8)
    tr = _pick(l_prompt, (CCONV_ROWS, 64, 32))
    tc = _pick(d, (512, 256, 128))
    tps = l_prompt // tr
    b3 = bdw.reshape(-1, 1, d)
    g3 = lng.reshape(-1, 1, d)
    l3 = lnb.reshape(-1, 1, d)
    vec = lambda i, j: (layer_j, 0, j)
    full = lambda i, j: (layer_j, 0, 0)
    zc = pl.pallas_call(
        functools.partial(_cconv_prompt_body, tiles_per_seq=tps, tc=tc),
        out_shape=jax.ShapeDtypeStruct((m, d), BF16),
        grid=(n_prompt * tps, d // tc),
        in_specs=[pl.BlockSpec((tr, tc), lambda i, j: (i, j)),
                  pl.BlockSpec((halo, tc), lambda i, j: (jnp.maximum(i * (tr // halo) - 1, 0), j)),
                  pl.BlockSpec((None, kw, tc), vec), pl.BlockSpec((None, 1, tc), vec),
                  pl.BlockSpec((None, 1, d), full), pl.BlockSpec((None, 1, d), full)],
        out_specs=pl.BlockSpec((tr, d), lambda i, j: (i, 0)),
        scratch_shapes=[pltpu.VMEM((1, tr + halo, tc), F32), pltpu.VMEM((1, tr + shift_pad, tc), F32),
                        pltpu.VMEM((d // tc, 1, tr, tc), F32)],
        compiler_params=_params(2), name="cconv_prompt",
    )(u, u, wdw, b3, g3, l3)

    bb = _pick(n_sample, (8, 4, 2, 1))
    r0 = (n_prompt * l_prompt) // (bb * l_sample)
    hb = kw - 1
    ext_rows = 8 * (((-hb) % 8 + hb + l_sample + 7) // 8)
    zc = pl.pallas_call(
        functools.partial(_cconv_sample_body, bb=bb, ls=l_sample, tc=tc),
        out_shape=jax.ShapeDtypeStruct((m, d), BF16),
        grid=(n_sample // bb, d // tc),
        in_specs=[pl.BlockSpec((bb * l_sample, tc), lambda i, j: (r0 + i, j)),
                  pl.BlockSpec((None, bb, hb, tc), lambda i, j: (layer_j, i, 0, j)),
                  pl.BlockSpec((None, kw, tc), vec), pl.BlockSpec((None, 1, tc), vec),
                  pl.BlockSpec((None, 1, d), full), pl.BlockSpec((None, 1, d), full),
                  pl.BlockSpec(memory_space=pl.ANY)],
        out_specs=pl.BlockSpec((bb * l_sample, d), lambda i, j: (r0 + i, 0)),
        scratch_shapes=[pltpu.VMEM((bb, ext_rows, tc), F32),
                        pltpu.VMEM((bb, l_sample + shift_pad, tc), F32),
                        pltpu.VMEM((d // tc, bb, l_sample, tc), F32)],
        input_output_aliases={6: 0}, compiler_params=_params(2), name="cconv_sample",
    )(u, state, wdw, b3, g3, l3, zc)
    return zc


def kernel(x_prompt, x_sample, state_gla, state_sconv, state_cconv, cache_mem_k, cache_mem_v, mem_prompt, norm_mix, norm_mem, norm_ffn, norm_final, gla_wq, gla_wk, gla_wv, gla_wg1, gla_wg2, gla_bg, gla_wr, gla_norm, gla_wo, sc_win, sc_wconv, sc_wout, cf_w1, cf_b1, cf_wdw, cf_bdw, cf_ln_g, cf_ln_b, cf_w2, cf_b2, mem_wq, mem_wk, mem_wv, mem_wo, ffn_up, ffn_down):
    n_prompt, l_prompt, d = x_prompt.shape
    n_sample, l_sample, _ = x_sample.shape
    depth = norm_mix.shape[0]
    n_mixers = 3
    mp, ms = n_prompt * l_prompt, n_sample * l_sample
    seq = dict(n_prompt=n_prompt, l_prompt=l_prompt, n_sample=n_sample, l_sample=l_sample)
    _, _, mem_len, mem_heads, mem_hd = cache_mem_k.shape
    mem_dim = mem_heads * mem_hd
    n_gla, _, gla_heads, hk, hv = state_gla.shape
    kw_sc = sc_wconv.shape[1]
    kw_cf = cf_wdw.shape[1]

    mem2 = mem_prompt.reshape(n_prompt * mem_len, d)
    kps = [_matmul(mem2, mem_wk, i, kind="plain", name="mem_k", tm=512) for i in range(depth)]
    vps = [_matmul(mem2, mem_wv, i, kind="plain", name="mem_v", tm=512) for i in range(depth)]

    rank = gla_wg1.shape[-1]
    rpad = LANES * ((rank + LANES - 1) // LANES)
    wg1p = jnp.pad(gla_wg1, ((0, 0), (0, 0), (0, rpad - rank)))
    wg2p = jnp.pad(gla_wg2, ((0, 0), (0, rpad - rank), (0, 0)))

    x, xb, ss = _prep(x_prompt.reshape(mp, d), x_sample.reshape(ms, d), norm_mix, 0)

    gla_p = gla_s = None
    sc_us, cf_us = [], []
    for i in range(depth):
        kind, j = i % n_mixers, i // n_mixers
        if kind == 0:
            q = _matmul(xb, gla_wq, j, kind="plain", name="gla_q", ss=ss, tm=WIDE_ROWS)
            k = _matmul(xb, gla_wk, j, kind="plain", name="gla_k", ss=ss, tm=WIDE_ROWS)
            v = _matmul(xb, gla_wv, j, kind="plain", name="gla_v", ss=ss, tm=WIDE_ROWS)
            rg = _matmul(xb, gla_wr, j, kind="plain", name="gla_r", ss=ss, tm=WIDE_ROWS)
            g1 = _matmul(xb, wg1p, j, kind="plain", name="gla_g1", ss=ss, tn=LANES)
            mix, gla_p, gla_s = _gla_scan(q, k, v, g1, rg, wg2p, gla_bg, gla_norm, state_gla, j,
                                          gla_p, gla_s, **seq)
            x, xb, ss = _matmul(mix, gla_wo, j, kind="delta", name="gla_o", res=x,
                                gain=norm_mem, gain_layer=i)
        elif kind == 1:
            bgate, u = _matmul(xb, sc_win, j, kind="sconv", name="sc_in", groups=3, ss=ss, tn=256)
            sc_us.append(u)
            z = _short_conv(u, bgate, sc_wconv, j, state_sconv, **seq)
            x, xb, ss = _matmul(z, sc_wout, j, kind="delta", name="sc_out", res=x,
                                gain=norm_mem, gain_layer=i)
        else:
            u = _matmul(xb, cf_w1, j, kind="glu", name="cf_in", groups=2, ss=ss, bias=cf_b1,
                        bias_layer=j, tm=WIDE_ROWS, tn=256)
            cf_us.append(u)
            zc = _conformer_conv(u, cf_wdw, cf_bdw, cf_ln_g, cf_ln_b, j, state_cconv, **seq)
            x, xb, ss = _matmul(zc, cf_w2, j, kind="delta", name="cf_out", res=x, bias=cf_b2,
                                bias_layer=j, gain=norm_mem, gain_layer=i)

        q = _matmul(xb, mem_wq, i, kind="plain", name="mem_q", ss=ss)
        o = _mem_attention(q, kps[i], vps[i], cache_mem_k, cache_mem_v, i, **seq)
        x, xb, ss = _matmul(o, mem_wo, i, kind="delta", name="mem_o", res=x, gain=norm_ffn,
                            gain_layer=i, tn=1024)

        hid = _matmul(xb, ffn_up, i, kind="relu2", name="ffn_up", ss=ss, out_dtype=BF16, tm=WIDE_ROWS)
        last = i == depth - 1
        x, xb, ss = _matmul(hid, ffn_down, i, kind="delta", name="ffn_down", res=x,
                            gain=norm_final.reshape(1, d) if last else norm_mix,
                            gain_layer=0 if last else i + 1, tm=FFN_DOWN_ROWS, tn=1024, tk=1024)

    y_prompt = _final_norm(x, ss, norm_final, 0, mp, "final_prompt").reshape(n_prompt, l_prompt, d)
    y_sample = _final_norm(x, ss, norm_final, mp, ms, "final_sample").reshape(n_sample, l_sample, d)

    def tails(us_, states, keep):
        tails_p, tails_s = [], []
        lead = max(l_prompt - keep, 0)
        for jj, u in enumerate(us_):
            up = jnp.stack([u[b * l_prompt + lead:(b + 1) * l_prompt] for b in range(n_prompt)])
            us = u[mp:].reshape(n_sample, l_sample, d)
            tails_p.append(jnp.concatenate([jnp.zeros((n_prompt, keep, d), F32), up], axis=1)[:, -keep:])
            tails_s.append(jnp.concatenate([states[jj], us], axis=1)[:, -keep:])
        return jnp.stack(tails_p), jnp.stack(tails_s)

    sc_p, sc_s = tails(sc_us, state_sconv, kw_sc - 1)
    cf_p, cf_s = tails(cf_us, state_cconv, kw_cf - 1)
    mem_k_p = jnp.stack(kps).reshape(depth, n_prompt, mem_len, mem_heads, mem_hd)
    mem_v_p = jnp.stack(vps).reshape(depth, n_prompt, mem_len, mem_heads, mem_hd)
    return (y_prompt, y_sample, gla_p, gla_s, sc_p, sc_s, cf_p, cf_s, mem_k_p, mem_v_p)
```

```python
import functools

import jax
import jax.numpy as jnp
from jax import lax
from jax.experimental import pallas as pl
from jax.experimental.pallas import tpu as pltpu

F32 = jnp.float32
BF16 = jnp.bfloat16

EPS = 1e-6
GLA_TAU = 16.0
GLA_SUB = 16
GLA_CHUNK = 128
GLA_PROMPT_HEADS = 2
V7X_VMEM_LIMIT_BYTES = 56 * 1024 * 1024
LANES = 128
MM_ROWS = 1024
WIDE_ROWS = 1536
ROW_TILES = (512, 256, 128, 64, 32, 16, 8)
SCONV_ROWS = 512
CCONV_ROWS = 128
ATTN_ROWS = 512


def _params(n_axes):
    return pltpu.CompilerParams(
        dimension_semantics=("arbitrary",) * n_axes,
        vmem_limit_bytes=V7X_VMEM_LIMIT_BYTES,
    )


def _dot(a, b):
    return jnp.dot(a, b, preferred_element_type=F32)


def _dot_nt(a, b):
    return lax.dot_general(a, b, (((1,), (1,)), ((), ())), preferred_element_type=F32)


def _dot_tn(a, b):
    return lax.dot_general(a, b, (((0,), (0,)), ((), ())), preferred_element_type=F32)


def _sigmoid(x):
    return 1.0 / (1.0 + jnp.exp(-x))


def _pick(n, prefs):
    for p in prefs:
        if p <= n and n % p == 0:
            return p
    return n


def _mm_body(*refs, groups, nk, kind, has_ss, has_bias, has_side, inv_d):
    it = iter(refs)
    a_ref = next(it)
    w_refs = [next(it) for _ in range(groups)]
    ss_ref = next(it) if has_ss else None
    b_refs = [next(it) for _ in range(groups)] if has_bias else None
    if kind == "delta":
        res_ref = next(it)
        gain_ref = next(it)
    side_in_ref = next(it) if has_side else None
    n_out = {"plain": 1, "relu2": 1, "glu": 1, "sconv": 2, "delta": 3}[kind]
    out_refs = [next(it) for _ in range(n_out)]
    side_out_ref = next(it) if has_side else None
    acc_refs = [next(it) for _ in range(groups)] if nk > 1 and kind != "delta" else None

    j = pl.program_id(1)
    k = pl.program_id(2)
    if has_side:
        side_out_ref[...] = side_in_ref[...].astype(BF16)

    def products():
        a = a_ref[...]
        if a.dtype != BF16:
            a = a.astype(BF16)
        return [_dot(a, w_ref[...].astype(BF16)) for w_ref in w_refs]

    def epilogue(accs):
        if has_ss:
            r = lax.rsqrt(ss_ref[...] * inv_d + EPS)
            accs = [x * r for x in accs]
        if has_bias:
            accs = [x + b_ref[...] for x, b_ref in zip(accs, b_refs)]
        if kind == "plain":
            out_refs[0][...] = accs[0].astype(out_refs[0].dtype)
        elif kind == "relu2":
            h = jnp.maximum(accs[0], 0.0)
            out_refs[0][...] = (h * h).astype(out_refs[0].dtype)
        elif kind == "glu":
            out_refs[0][...] = accs[0] * _sigmoid(accs[1])
        else:
            assert kind == "sconv"
            out_refs[0][...] = accs[0]
            out_refs[1][...] = accs[1] * accs[2]

    def delta_tail(xn):
        out_refs[1][...] = (xn * gain_ref[...]).astype(BF16)
        s = jnp.sum(xn * xn, axis=1, keepdims=True)

        @pl.when(j == 0)
        def _():
            out_refs[2][...] = s

        @pl.when(j > 0)
        def _():
            out_refs[2][...] += s

    if nk == 1 and kind == "delta":
        xn = products()[0]
        if has_bias:
            xn = xn + b_refs[0][...]
        xn = res_ref[...] + xn
        out_refs[0][...] = xn
        delta_tail(xn)
    elif nk == 1:
        epilogue(products())
    elif kind == "delta":
        @pl.when(k == 0)
        def _():
            seed = res_ref[...]
            if has_bias:
                seed = seed + b_refs[0][...]
            out_refs[0][...] = seed

        out_refs[0][...] = out_refs[0][...] + products()[0]

        @pl.when(k == nk - 1)
        def _():
            delta_tail(out_refs[0][...])
    else:
        @pl.when(k == 0)
        def _():
            for acc_ref in acc_refs:
                acc_ref[...] = jnp.zeros_like(acc_ref)

        for acc_ref, p in zip(acc_refs, products()):
            acc_ref[...] = acc_ref[...] + p

        @pl.when(k == nk - 1)
        def _():
            epilogue([acc_ref[...] for acc_ref in acc_refs])


def _matmul(a, w, layer, *, kind, name, groups=1, ss=None, bias=None, bias_layer=0,
            res=None, gain=None, gain_layer=0, out_dtype=F32, tm=None, tn=512, tk=None,
            side_w=None, side_layer=0):
    m, kdim = a.shape
    n = w.shape[-1] // groups
    tm = _pick(m, (MM_ROWS if tm is None else tm,) + ROW_TILES)
    tn = _pick(n, (tn, 512, 256, 128))
    tk = kdim if tk is None else _pick(kdim, (tk, 1024, 512, 256, 128))
    nk = kdim // tk
    nj = n // tn
    grid = (m // tm, nj, nk)
    has_ss = ss is not None
    has_bias = bias is not None

    in_specs = [pl.BlockSpec((tm, tk), lambda i, j, k: (i, k))]
    operands = [a]
    for g in range(groups):
        in_specs.append(pl.BlockSpec((None, tk, tn), lambda i, j, k, g=g: (layer, k, g * nj + j)))
        operands.append(w)
    if has_ss:
        in_specs.append(pl.BlockSpec((tm, 1), lambda i, j, k: (i, 0)))
        operands.append(ss)
    if has_bias:
        b3 = bias.reshape(bias.shape[0], 1, bias.shape[-1])
        for g in range(groups):
            in_specs.append(pl.BlockSpec((None, 1, tn), lambda i, j, k, g=g: (bias_layer, 0, g * nj + j)))
            operands.append(b3)
    tile = pl.BlockSpec((tm, tn), lambda i, j, k: (i, j))
    if kind == "delta":
        g3 = gain.reshape(-1, 1, gain.shape[-1])
        in_specs += [tile, pl.BlockSpec((None, 1, tn), lambda i, j, k: (gain_layer, 0, j))]
        operands += [res, g3]
        out_shape = [jax.ShapeDtypeStruct((m, n), F32), jax.ShapeDtypeStruct((m, n), BF16),
                     jax.ShapeDtypeStruct((m, 1), F32)]
        out_specs = [tile, tile, pl.BlockSpec((tm, 1), lambda i, j, k: (i, 0))]
    elif kind == "sconv":
        out_shape = [jax.ShapeDtypeStruct((m, n), F32)] * 2
        out_specs = [tile, tile]
    else:
        out_shape = [jax.ShapeDtypeStruct((m, n), out_dtype)]
        out_specs = [tile]
    scratch = [pltpu.VMEM((tm, tn), F32) for _ in range(groups)] if nk > 1 and kind != "delta" else []

    has_side = side_w is not None
    if has_side:
        _, k2, n2 = side_w.shape
        steps = grid[0] * nj * nk
        sr = min([r for r in ROW_TILES if k2 % r == 0 and k2 // r <= steps] or [k2])
        last = k2 // sr - 1
        in_specs.append(pl.BlockSpec(
            (None, sr, n2), lambda i, j, k: (side_layer, jnp.minimum((i * nj + j) * nk + k, last), 0)))
        operands.append(side_w)
        out_shape = out_shape + [jax.ShapeDtypeStruct((k2, n2), BF16)]
        out_specs = out_specs + [pl.BlockSpec(
            (sr, n2), lambda i, j, k: (jnp.minimum((i * nj + j) * nk + k, last), 0))]

    body = functools.partial(_mm_body, groups=groups, nk=nk, kind=kind, has_ss=has_ss,
                             has_bias=has_bias, has_side=has_side, inv_d=1.0 / kdim)
    outs = pl.pallas_call(
        body, out_shape=out_shape, grid=grid, in_specs=in_specs, out_specs=out_specs,
        scratch_shapes=scratch, compiler_params=_params(3), name=name,
    )(*operands)
    return outs if len(outs) > 1 else outs[0]


def _prep_body(xp_ref, xs_ref, g_ref, x_ref, xb_ref, ss_ref, *, n_prompt_tiles):
    i = pl.program_id(0)

    def emit(x):
        x_ref[...] = x
        xb_ref[...] = (x * g_ref[...]).astype(BF16)
        ss_ref[...] = jnp.sum(x * x, axis=1, keepdims=True)

    @pl.when(i < n_prompt_tiles)
    def _():
        emit(xp_ref[...])

    @pl.when(i >= n_prompt_tiles)
    def _():
        emit(xs_ref[...])


def _prep(xp2, xs2, gain, gain_layer):
    mp, d = xp2.shape
    ms = xs2.shape[0]
    tr = _pick(ms, (256, 128, 64, 32, 16, 8))
    npt, nst = mp // tr, ms // tr
    m = mp + ms
    g3 = gain.reshape(-1, 1, d)
    row = pl.BlockSpec((tr, d), lambda i: (i, 0))
    return pl.pallas_call(
        functools.partial(_prep_body, n_prompt_tiles=npt),
        out_shape=[jax.ShapeDtypeStruct((m, d), F32), jax.ShapeDtypeStruct((m, d), BF16),
                   jax.ShapeDtypeStruct((m, 1), F32)],
        grid=(npt + nst,),
        in_specs=[pl.BlockSpec((tr, d), lambda i: (jnp.minimum(i, npt - 1), 0)),
                  pl.BlockSpec((tr, d), lambda i: (jnp.maximum(i - npt, 0), 0)),
                  pl.BlockSpec((None, 1, d), lambda i: (gain_layer, 0, 0))],
        out_specs=[row, row, pl.BlockSpec((tr, 1), lambda i: (i, 0))],
        compiler_params=_params(1), name="prep",
    )(xp2, xs2, g3)


def _final_body(x_ref, ss_ref, g_ref, y_ref, *, inv_d):
    y_ref[...] = x_ref[...] * lax.rsqrt(ss_ref[...] * inv_d + EPS) * g_ref[...]


def _final_norm(x, ss, gain, row0, rows, name):
    d = x.shape[1]
    tr = _pick(rows, (256, 128, 64, 32, 16, 8))
    off = row0 // tr
    return pl.pallas_call(
        functools.partial(_final_body, inv_d=1.0 / d),
        out_shape=jax.ShapeDtypeStruct((rows, d), F32),
        grid=(rows // tr,),
        in_specs=[pl.BlockSpec((tr, d), lambda i: (off + i, 0)),
                  pl.BlockSpec((tr, 1), lambda i: (off + i, 0)),
                  pl.BlockSpec((1, d), lambda i: (0, 0))],
        out_specs=pl.BlockSpec((tr, d), lambda i: (i, 0)),
        compiler_params=_params(1), name=name,
    )(x, ss, gain.reshape(1, d))


def _log_sigmoid(x):
    return jnp.minimum(x, 0.0) - jnp.log1p(jnp.exp(-jnp.abs(x)))


def _split3(x):
    hi = x.astype(BF16)
    r1 = x - hi.astype(F32)
    mid = r1.astype(BF16)
    lo = (r1 - mid.astype(F32)).astype(BF16)
    return hi, mid, lo


def _gla_log_decay(g1, wg2, bg):
    return _log_sigmoid(_dot(g1.astype(BF16), wg2.astype(BF16)) + bg) * (1.0 / GLA_TAU)


def _gla_chunk(q, k, v, g, s, *, sub, scale):
    c, hk = q.shape

    row = lax.broadcasted_iota(jnp.int32, (c, c), 0)
    col = lax.broadcasted_iota(jnp.int32, (c, c), 1)
    causal = col <= row
    tril = jnp.where(causal, 1.0, 0.0).astype(BF16)
    g_parts = _split3(g)
    b = _dot(tril, g_parts[0]) + _dot(tril, g_parts[1]) + _dot(tril, g_parts[2])
    ones = jnp.ones((c, LANES), BF16)
    b_last_col = (_dot_tn(g_parts[0], ones) + _dot_tn(g_parts[1], ones)
                  + _dot_tn(g_parts[2], ones))[:, :1]
    b_last = b[c - 1:c, :]

    qs = q * scale
    vb = v.astype(BF16)
    o = _dot((qs * jnp.exp(b)).astype(BF16), s.astype(BF16))

    key_pos = lax.broadcasted_iota(jnp.int32, (c, 1), 0)
    att_rows = []
    for i in range(c // sub):
        lo, hi = i * sub, (i + 1) * sub
        b_ref = b[lo - 1:lo, :] if i > 0 else jnp.zeros((1, hk), F32)
        qi = (qs[lo:hi] * jnp.exp(b[lo:hi] - b_ref)).astype(BF16)
        expo = jnp.where(key_pos < hi, b_ref - b, -1e30)
        kk = (k * jnp.exp(expo)).astype(BF16)
        att_rows.append(_dot_nt(qi, kk))
    att = att_rows[0] if len(att_rows) == 1 else jnp.concatenate(att_rows, axis=0)
    att = jnp.where(causal, att, 0.0)
    o = o + _dot(att.astype(BF16), vb)

    kd = (k * jnp.exp(b_last - b)).astype(BF16)
    s_new = jnp.exp(b_last_col) * s + _dot_tn(kd, vb)
    return o, s_new


def _gla_gate_out(o, rg, ng):
    on = o * lax.rsqrt(jnp.mean(o * o, axis=-1, keepdims=True) + EPS) * ng
    return (on * (rg * _sigmoid(rg))).astype(BF16)


def _gla_prompt_body(q_ref, k_ref, v_ref, g1_ref, wg2_ref, bg_ref, rg_ref, ng_ref,
                     og_ref, sout_ref, s_ref, *, nc, scale):
    c = pl.program_id(2)

    @pl.when(c == 0)
    def _():
        s_ref[...] = jnp.zeros_like(s_ref)

    g = _gla_log_decay(g1_ref[...], wg2_ref[...], bg_ref[...])
    hk, hv = s_ref.shape[1], s_ref.shape[2]
    for h in range(s_ref.shape[0]):
        ck = slice(h * hk, (h + 1) * hk)
        cv = slice(h * hv, (h + 1) * hv)
        o, s_new = _gla_chunk(q_ref[:, ck], k_ref[:, ck], v_ref[:, cv], g[:, ck], s_ref[h],
                              sub=GLA_SUB, scale=scale)
        s_ref[h] = s_new
        og_ref[:, cv] = _gla_gate_out(o, rg_ref[:, cv], ng_ref[:, cv])

    @pl.when(c == nc - 1)
    def _():
        sout_ref[...] = s_ref[...]


def _gla_sample_body(q_ref, k_ref, v_ref, g1_ref, wg2_ref, bg_ref, rg_ref, ng_ref, s0_ref, *rest,
                     scale, sub, nh):
    og_ref, sout_ref = rest[-2], rest[-1]
    hk, hv = s0_ref.shape[1], s0_ref.shape[2]
    g = _gla_log_decay(g1_ref[...], wg2_ref[...], bg_ref[...])
    for h in range(nh):
        ck = slice(h * hk, (h + 1) * hk)
        cv = slice(h * hv, (h + 1) * hv)
        o, s_new = _gla_chunk(q_ref[:, ck], k_ref[:, ck], v_ref[:, cv], g[:, ck], s0_ref[h],
                              sub=sub, scale=scale)
        og_ref[:, cv] = _gla_gate_out(o, rg_ref[:, cv], ng_ref[:, cv])
        sout_ref[h] = s_new


def _gla_scan(q, k, v, g1, rg, wg2p, bg, ng, state_in, layer_j, sp_prev, gs_prev, *,
              n_prompt, l_prompt, n_sample, l_sample):
    n_gla, _, nh, hk, hv = state_in.shape
    m = q.shape[0]
    scale = float(hk) ** -0.5
    rk = g1.shape[1]
    wg3 = wg2p
    bg3 = bg.reshape(n_gla, 1, nh * hk)
    ng3 = ng.reshape(n_gla, 1, nh * hv)

    cc = _pick(l_prompt, (GLA_CHUNK, 32, 16, 8))
    nc = l_prompt // cc
    sp_shape = jax.ShapeDtypeStruct((n_gla, n_prompt, nh, hk, hv), F32)
    hps = _pick(nh, (GLA_PROMPT_HEADS, 1))
    rowsel = lambda b, h, c: (b * nc + c, h)
    in_specs = [pl.BlockSpec((cc, hps * hk), rowsel), pl.BlockSpec((cc, hps * hk), rowsel),
                pl.BlockSpec((cc, hps * hv), rowsel),
                pl.BlockSpec((cc, rk), lambda b, h, c: (b * nc + c, 0)),
                pl.BlockSpec((None, rk, hps * hk), lambda b, h, c: (layer_j, 0, h)),
                pl.BlockSpec((None, 1, hps * hk), lambda b, h, c: (layer_j, 0, h)),
                pl.BlockSpec((cc, hps * hv), rowsel),
                pl.BlockSpec((None, 1, hps * hv), lambda b, h, c: (layer_j, 0, h))]
    operands = [q, k, v, g1, wg3, bg3, rg, ng3]
    aliases = {}
    if sp_prev is not None:
        in_specs.append(pl.BlockSpec(memory_space=pl.ANY))
        operands.append(sp_prev)
        aliases = {len(operands) - 1: 1}

    def prompt_body(*refs):
        if sp_prev is not None:
            refs = refs[:8] + refs[9:]
        _gla_prompt_body(*refs, nc=nc, scale=scale)

    og, sp = pl.pallas_call(
        prompt_body,
        out_shape=[jax.ShapeDtypeStruct((m, nh * hv), BF16), sp_shape],
        grid=(n_prompt, nh // hps, nc), in_specs=in_specs,
        out_specs=[pl.BlockSpec((cc, hps * hv), rowsel),
                   pl.BlockSpec((None, None, hps, hk, hv), lambda b, h, c: (layer_j, b, h, 0, 0))],
        scratch_shapes=[pltpu.VMEM((hps, hk, hv), F32)],
        input_output_aliases=aliases, compiler_params=_params(3), name="gla_prompt",
    )(*operands)

    r0 = (n_prompt * l_prompt) // l_sample
    rows_s = lambda b: (r0 + b, 0)
    wsel = lambda b: (layer_j, 0, 0)
    state_spec = pl.BlockSpec((None, None, nh, hk, hv), lambda b: (layer_j, b, 0, 0, 0))
    in_specs = [pl.BlockSpec((l_sample, nh * hk), rows_s), pl.BlockSpec((l_sample, nh * hk), rows_s),
                pl.BlockSpec((l_sample, nh * hv), rows_s),
                pl.BlockSpec((l_sample, rk), rows_s),
                pl.BlockSpec((None, rk, nh * hk), wsel),
                pl.BlockSpec((None, 1, nh * hk), wsel),
                pl.BlockSpec((l_sample, nh * hv), rows_s),
                pl.BlockSpec((None, 1, nh * hv), wsel),
                state_spec,
                pl.BlockSpec(memory_space=pl.ANY)]
    operands = [q, k, v, g1, wg3, bg3, rg, ng3, state_in, og]
    aliases = {9: 0}
    if gs_prev is not None:
        in_specs.append(pl.BlockSpec(memory_space=pl.ANY))
        operands.append(gs_prev)
        aliases[10] = 1
    og, gs_new = pl.pallas_call(
        functools.partial(_gla_sample_body, scale=scale, sub=min(GLA_SUB, l_sample), nh=nh),
        out_shape=[jax.ShapeDtypeStruct((m, nh * hv), BF16),
                   jax.ShapeDtypeStruct((n_gla, n_sample, nh, hk, hv), F32)],
        grid=(n_sample,), in_specs=in_specs,
        out_specs=[pl.BlockSpec((l_sample, nh * hv), rows_s), state_spec],
        input_output_aliases=aliases, compiler_params=_params(1), name="gla_sample",
    )(*operands)
    return og, sp, gs_new


def _softmax_rows(s):
    e = jnp.exp(s - jnp.max(s, axis=-1, keepdims=True))
    return e / jnp.sum(e, axis=-1, keepdims=True)


def _attn_prompt_body(q_ref, k_ref, v_ref, o_ref, *, scale, nh, hd):
    for h in range(nh):
        cols = slice(h * hd, (h + 1) * hd)
        q = q_ref[:, cols].astype(BF16)
        p = _softmax_rows(_dot_nt(q, k_ref[:, cols].astype(BF16)) * scale)
        o_ref[:, cols] = _dot(p.astype(BF16), v_ref[:, cols].astype(BF16)).astype(BF16)


def _attn_sample_body(q_ref, k_ref, v_ref, o_in_ref, o_ref, *, scale, bb, ls):
    del o_in_ref
    nh, hd = k_ref.shape[2], k_ref.shape[3]
    kh = pltpu.einshape("bmhd->hbmd", k_ref[...])
    vh = pltpu.einshape("bmhd->hbmd", v_ref[...])
    for h in range(nh):
        cols = slice(h * hd, (h + 1) * hd)
        q = q_ref[:, cols].reshape(bb, ls, hd).astype(BF16)
        s = lax.dot_general(q, kh[h].astype(BF16), (((2,), (2,)), ((0,), (0,))),
                            preferred_element_type=F32) * scale
        p = _softmax_rows(s).astype(BF16)
        o = lax.dot_general(p, vh[h].astype(BF16), (((2,), (1,)), ((0,), (0,))),
                            preferred_element_type=F32)
        o_ref[:, cols] = o.reshape(bb * ls, hd).astype(BF16)


def _mem_attention(q, kp, vp, cache_k, cache_v, layer, *, n_prompt, l_prompt, n_sample, l_sample):
    m, dm = q.shape
    _, _, mem_len, nh, hd = cache_k.shape
    scale = float(hd) ** -0.5
    tq = _pick(l_prompt, (ATTN_ROWS,) + ROW_TILES)
    nt = l_prompt // tq
    o = pl.pallas_call(
        functools.partial(_attn_prompt_body, scale=scale, nh=nh, hd=hd),
        out_shape=jax.ShapeDtypeStruct((m, dm), BF16),
        grid=(n_prompt, nt),
        in_specs=[pl.BlockSpec((tq, dm), lambda b, t: (b * nt + t, 0)),
                  pl.BlockSpec((mem_len, dm), lambda b, t: (b, 0)),
                  pl.BlockSpec((mem_len, dm), lambda b, t: (b, 0))],
        out_specs=pl.BlockSpec((tq, dm), lambda b, t: (b * nt + t, 0)),
        compiler_params=_params(2), name="attn_prompt",
    )(q, kp, vp)

    bb = _pick(n_sample, (8, 4, 2, 1))
    r0 = (n_prompt * l_prompt) // (bb * l_sample)
    cache_spec = pl.BlockSpec((None, bb, mem_len, nh, hd), lambda i: (layer, i, 0, 0, 0))
    o = pl.pallas_call(
        functools.partial(_attn_sample_body, scale=scale, bb=bb, ls=l_sample),
        out_shape=jax.ShapeDtypeStruct((m, dm), BF16),
        grid=(n_sample // bb,),
        in_specs=[pl.BlockSpec((bb * l_sample, dm), lambda i: (r0 + i, 0)),
                  cache_spec, cache_spec,
                  pl.BlockSpec(memory_space=pl.ANY)],
        out_specs=pl.BlockSpec((bb * l_sample, dm), lambda i: (r0 + i, 0)),
        input_output_aliases={3: 0}, compiler_params=_params(1), name="attn_sample",
    )(q, cache_k, cache_v, o)
    return o


def _shift_rows(u, hist_rows, t, axis):
    n = len(hist_rows)
    out = []
    for s in range(1, n + 1):
        shifted = pltpu.roll(u, s, axis=axis)
        for tt in range(s):
            shifted = jnp.where(t == tt, hist_rows[n - s + tt], shifted)
        out.append(shifted)
    return out


def _sconv_prompt_body(u_ref, prev_ref, bg_ref, w_ref, z_ref, *, tiles_per_seq):
    i = pl.program_id(0)
    u = u_ref[...]
    keep = jnp.where(i % tiles_per_seq == 0, 0.0, 1.0)
    prev = prev_ref[...] * keep
    kw = w_ref.shape[0]
    hist = [prev[8 - (kw - 1) + r: 8 - (kw - 1) + r + 1, :] for r in range(kw - 1)]
    t = lax.broadcasted_iota(jnp.int32, (u.shape[0], 1), 0)
    shifted = _shift_rows(u, hist, t, 0)
    y = w_ref[kw - 1:kw, :] * u
    for s, us in enumerate(shifted, start=1):
        y = y + w_ref[kw - 1 - s:kw - s, :] * us
    z_ref[...] = (bg_ref[...] * y).astype(BF16)


def _sconv_sample_body(u_ref, hist_ref, bg_ref, w_ref, z_in_ref, z_ref, *, bb, ls):
    del z_in_ref
    tc = u_ref.shape[1]
    u = u_ref[...].reshape(bb, ls, tc)
    kw = w_ref.shape[0]
    hist = [hist_ref[:, r:r + 1, :] for r in range(kw - 1)]
    t = lax.broadcasted_iota(jnp.int32, (1, ls, 1), 1)
    shifted = _shift_rows(u, hist, t, 1)
    y = w_ref[kw - 1:kw, :] * u
    for s, us in enumerate(shifted, start=1):
        y = y + w_ref[kw - 1 - s:kw - s, :] * us
    z_ref[...] = (bg_ref[...] * y.reshape(bb * ls, tc)).astype(BF16)


def _short_conv(u, bgate, wconv, layer_j, state, *, n_prompt, l_prompt, n_sample, l_sample):
    m, d = u.shape
    kw = wconv.shape[1]
    tr = _pick(l_prompt, (SCONV_ROWS,) + ROW_TILES)
    tc = _pick(d, (512, 256, 128))
    tps = l_prompt // tr
    z = pl.pallas_call(
        functools.partial(_sconv_prompt_body, tiles_per_seq=tps),
        out_shape=jax.ShapeDtypeStruct((m, d), BF16),
        grid=(n_prompt * tps, d // tc),
        in_specs=[pl.BlockSpec((tr, tc), lambda i, j: (i, j)),
                  pl.BlockSpec((8, tc), lambda i, j: (jnp.maximum(i * (tr // 8) - 1, 0), j)),
                  pl.BlockSpec((tr, tc), lambda i, j: (i, j)),
                  pl.BlockSpec((None, kw, tc), lambda i, j: (layer_j, 0, j))],
        out_specs=pl.BlockSpec((tr, tc), lambda i, j: (i, j)),
        compiler_params=_params(2), name="sconv_prompt",
    )(u, u, bgate, wconv)

    bb = _pick(n_sample, (64, 32, 16, 8, 4, 2, 1))
    r0 = (n_prompt * l_prompt) // (bb * l_sample)
    z = pl.pallas_call(
        functools.partial(_sconv_sample_body, bb=bb, ls=l_sample),
        out_shape=jax.ShapeDtypeStruct((m, d), BF16),
        grid=(n_sample // bb, d // tc),
        in_specs=[pl.BlockSpec((bb * l_sample, tc), lambda i, j: (r0 + i, j)),
                  pl.BlockSpec((None, bb, kw - 1, tc), lambda i, j: (layer_j, i, 0, j)),
                  pl.BlockSpec((bb * l_sample, tc), lambda i, j: (r0 + i, j)),
                  pl.BlockSpec((None, kw, tc), lambda i, j: (layer_j, 0, j)),
                  pl.BlockSpec(memory_space=pl.ANY)],
        out_specs=pl.BlockSpec((bb * l_sample, tc), lambda i, j: (r0 + i, j)),
        input_output_aliases={4: 0}, compiler_params=_params(2), name="sconv_sample",
    )(u, state, bgate, wconv, z)
    return z


def _cconv_core(u3, hist3, w_ref, bdw_ref, lg_ref, lb_ref, ext_ref, shift_ref, z_ref, out_ref, *, tc):
    j = pl.program_id(1)
    nj = pl.num_programs(1)
    bb, ll, _ = u3.shape
    hb = hist3.shape[1]
    kw = w_ref.shape[0]
    off = (-hb) % 8
    ext_ref[:, off:off + hb, :] = hist3
    ext_ref[:, off + hb:off + hb + ll, :] = u3
    base = off + hb - (kw - 1)
    acc = jnp.zeros((bb, ll, tc), F32) + bdw_ref[...]
    for r in range(min(8, kw)):
        n_a = (kw - 1 - r) // 8 + 1
        rows = ll + 8 * (n_a - 1)
        shift_ref[:, 0:rows, :] = ext_ref[:, base + r:base + r + rows, :]
        for a in range(n_a):
            kk = 8 * a + r
            acc = acc + w_ref[kk:kk + 1, :] * shift_ref[:, 8 * a:8 * a + ll, :]
    z_ref[j] = acc

    @pl.when(j == nj - 1)
    def _():
        n_tiles = z_ref.shape[0]
        inv_d = 1.0 / (n_tiles * tc)
        z = z_ref[...]
        mu = jnp.sum(jnp.sum(z, axis=3, keepdims=True), axis=0, keepdims=True) * inv_d
        zc = z - mu
        var = jnp.sum(jnp.sum(zc * zc, axis=3, keepdims=True), axis=0, keepdims=True) * inv_d
        zn = zc * lax.rsqrt(var + EPS)
        for jj in range(n_tiles):
            cols = slice(jj * tc, (jj + 1) * tc)
            y = zn[jj] * lg_ref[:, cols] + lb_ref[:, cols]
            y = y * _sigmoid(y)
            out_ref[:, cols] = y.reshape(bb * ll, tc).astype(BF16)


def _cconv_prompt_body(u_ref, halo_ref, w_ref, bdw_ref, lg_ref, lb_ref, out_ref, ext_ref, shift_ref,
                       z_ref, *, tiles_per_seq, tc):
    i = pl.program_id(0)
    keep = jnp.where(i % tiles_per_seq == 0, 0.0, 1.0)
    u3 = u_ref[...][None]
    hist3 = (halo_ref[...] * keep)[None]
    _cconv_core(u3, hist3, w_ref, bdw_ref, lg_ref, lb_ref, ext_ref, shift_ref, z_ref, out_ref, tc=tc)


def _cconv_sample_body(u_ref, hist_ref, w_ref, bdw_ref, lg_ref, lb_ref, o_in_ref, out_ref, ext_ref,
                       shift_ref, z_ref, *, bb, ls, tc):
    del o_in_ref
    u3 = u_ref[...].reshape(bb, ls, tc)
    _cconv_core(u3, hist_ref[...], w_ref, bdw_ref, lg_ref, lb_ref, ext_ref, shift_ref, z_ref, out_ref,
                tc=tc)


def _conformer_conv(u, wdw, bdw, lng, lnb, layer_j, state, *, n_prompt, l_prompt, n_sample, l_sample):
    m, d = u.shape
    kw = wdw.shape[1]
    halo = 8 * ((kw - 1 + 7) // 8)
    shift_pad = 8 * ((kw - 1) // 8)
    tr = _pick(l_prompt, (CCONV_ROWS, 64, 32))
    tc = _pick(d, (512, 256, 128))
    tps = l_prompt // tr
    b3 = bdw.reshape(-1, 1, d)
    g3 = lng.reshape(-1, 1, d)
    l3 = lnb.reshape(-1, 1, d)
    vec = lambda i, j: (layer_j, 0, j)
    full = lambda i, j: (layer_j, 0, 0)
    zc = pl.pallas_call(
        functools.partial(_cconv_prompt_body, tiles_per_seq=tps, tc=tc),
        out_shape=jax.ShapeDtypeStruct((m, d), BF16),
        grid=(n_prompt * tps, d // tc),
        in_specs=[pl.BlockSpec((tr, tc), lambda i, j: (i, j)),
                  pl.BlockSpec((halo, tc), lambda i, j: (jnp.maximum(i * (tr // halo) - 1, 0), j)),
                  pl.BlockSpec((None, kw, tc), vec), pl.BlockSpec((None, 1, tc), vec),
                  pl.BlockSpec((None, 1, d), full), pl.BlockSpec((None, 1, d), full)],
        out_specs=pl.BlockSpec((tr, d), lambda i, j: (i, 0)),
        scratch_shapes=[pltpu.VMEM((1, tr + halo, tc), F32), pltpu.VMEM((1, tr + shift_pad, tc), F32),
                        pltpu.VMEM((d // tc, 1, tr, tc), F32)],
        compiler_params=_params(2), name="cconv_prompt",
    )(u, u, wdw, b3, g3, l3)

    bb = _pick(n_sample, (8, 4, 2, 1))
    r0 = (n_prompt * l_prompt) // (bb * l_sample)
    hb = kw - 1
    ext_rows = 8 * (((-hb) % 8 + hb + l_sample + 7) // 8)
    zc = pl.pallas_call(
        functools.partial(_cconv_sample_body, bb=bb, ls=l_sample, tc=tc),
        out_shape=jax.ShapeDtypeStruct((m, d), BF16),
        grid=(n_sample // bb, d // tc),
        in_specs=[pl.BlockSpec((bb * l_sample, tc), lambda i, j: (r0 + i, j)),
                  pl.BlockSpec((None, bb, hb, tc), lambda i, j: (layer_j, i, 0, j)),
                  pl.BlockSpec((None, kw, tc), vec), pl.BlockSpec((None, 1, tc), vec),
                  pl.BlockSpec((None, 1, d), full), pl.BlockSpec((None, 1, d), full),
                  pl.BlockSpec(memory_space=pl.ANY)],
        out_specs=pl.BlockSpec((bb * l_sample, d), lambda i, j: (r0 + i, 0)),
        scratch_shapes=[pltpu.VMEM((bb, ext_rows, tc), F32),
                        pltpu.VMEM((bb, l_sample + shift_pad, tc), F32),
                        pltpu.VMEM((d // tc, bb, l_sample, tc), F32)],
        input_output_aliases={6: 0}, compiler_params=_params(2), name="cconv_sample",
    )(u, state, wdw, b3, g3, l3, zc)
    return zc


def kernel(x_prompt, x_sample, state_gla, state_sconv, state_cconv, cache_mem_k, cache_mem_v, mem_prompt, norm_mix, norm_mem, norm_ffn, norm_final, gla_wq, gla_wk, gla_wv, gla_wg1, gla_wg2, gla_bg, gla_wr, gla_norm, gla_wo, sc_win, sc_wconv, sc_wout, cf_w1, cf_b1, cf_wdw, cf_bdw, cf_ln_g, cf_ln_b, cf_w2, cf_b2, mem_wq, mem_wk, mem_wv, mem_wo, ffn_up, ffn_down):
    n_prompt, l_prompt, d = x_prompt.shape
    n_sample, l_sample, _ = x_sample.shape
    depth = norm_mix.shape[0]
    n_mixers = 3
    mp, ms = n_prompt * l_prompt, n_sample * l_sample
    seq = dict(n_prompt=n_prompt, l_prompt=l_prompt, n_sample=n_sample, l_sample=l_sample)
    _, _, mem_len, mem_heads, mem_hd = cache_mem_k.shape
    mem_dim = mem_heads * mem_hd
    n_gla, _, gla_heads, hk, hv = state_gla.shape
    kw_sc = sc_wconv.shape[1]
    kw_cf = cf_wdw.shape[1]

    mem2 = mem_prompt.reshape(n_prompt * mem_len, d)
    kps = [_matmul(mem2, mem_wk, i, kind="plain", name="mem_k", tm=512) for i in range(depth)]
    vps = [_matmul(mem2, mem_wv, i, kind="plain", name="mem_v", tm=512) for i in range(depth)]

    rank = gla_wg1.shape[-1]
    rpad = LANES * ((rank + LANES - 1) // LANES)
    wg1p = jnp.pad(gla_wg1, ((0, 0), (0, 0), (0, rpad - rank)))
    wg2p = jnp.pad(gla_wg2, ((0, 0), (0, rpad - rank), (0, 0)))

    x, xb, ss = _prep(x_prompt.reshape(mp, d), x_sample.reshape(ms, d), norm_mix, 0)

    gla_p = gla_s = None
    sc_us, cf_us = [], []
    for i in range(depth):
        kind, j = i % n_mixers, i // n_mixers
        if kind == 0:
            q = _matmul(xb, gla_wq, j, kind="plain", name="gla_q", ss=ss, tm=WIDE_ROWS)
            k = _matmul(xb, gla_wk, j, kind="plain", name="gla_k", ss=ss, tm=WIDE_ROWS)
            v = _matmul(xb, gla_wv, j, kind="plain", name="gla_v", ss=ss, tm=WIDE_ROWS)
            rg = _matmul(xb, gla_wr, j, kind="plain", name="gla_r", ss=ss, tm=WIDE_ROWS)
            g1 = _matmul(xb, wg1p, j, kind="plain", name="gla_g1", ss=ss, tn=LANES)
            mix, gla_p, gla_s = _gla_scan(q, k, v, g1, rg, wg2p, gla_bg, gla_norm, state_gla, j,
                                          gla_p, gla_s, **seq)
            x, xb, ss = _matmul(mix, gla_wo, j, kind="delta", name="gla_o", res=x,
                                gain=norm_mem, gain_layer=i)
        elif kind == 1:
            bgate, u = _matmul(xb, sc_win, j, kind="sconv", name="sc_in", groups=3, ss=ss, tn=256)
            sc_us.append(u)
            z = _short_conv(u, bgate, sc_wconv, j, state_sconv, **seq)
            x, xb, ss = _matmul(z, sc_wout, j, kind="delta", name="sc_out", res=x,
                                gain=norm_mem, gain_layer=i)
        else:
            u = _matmul(xb, cf_w1, j, kind="glu", name="cf_in", groups=2, ss=ss, bias=cf_b1,
                        bias_layer=j, tm=WIDE_ROWS, tn=256)
            cf_us.append(u)
            zc = _conformer_conv(u, cf_wdw, cf_bdw, cf_ln_g, cf_ln_b, j, state_cconv, **seq)
            x, xb, ss = _matmul(zc, cf_w2, j, kind="delta", name="cf_out", res=x, bias=cf_b2,
                                bias_layer=j, gain=norm_mem, gain_layer=i)

        q = _matmul(xb, mem_wq, i, kind="plain", name="mem_q", ss=ss)
        o = _mem_attention(q, kps[i], vps[i], cache_mem_k, cache_mem_v, i, **seq)
        x, xb, ss = _matmul(o, mem_wo, i, kind="delta", name="mem_o", res=x, gain=norm_ffn,
                            gain_layer=i, tn=1024)

        hid, w_down = _matmul(xb, ffn_up, i, kind="relu2", name="ffn_up", ss=ss, out_dtype=BF16,
                              side_w=ffn_down, side_layer=i)
        last = i == depth - 1
        x, xb, ss = _matmul(hid, w_down[None], 0, kind="delta", name="ffn_down", res=x,
                            gain=norm_final.reshape(1, d) if last else norm_mix,
                            gain_layer=0 if last else i + 1, tn=1024, tk=2048)

    y_prompt = _final_norm(x, ss, norm_final, 0, mp, "final_prompt").reshape(n_prompt, l_prompt, d)
    y_sample = _final_norm(x, ss, norm_final, mp, ms, "final_sample").reshape(n_sample, l_sample, d)

    def tails(us_, states, keep):
        tails_p, tails_s = [], []
        lead = max(l_prompt - keep, 0)
        for jj, u in enumerate(us_):
            up = jnp.stack([u[b * l_prompt + lead:(b + 1) * l_prompt] for b in range(n_prompt)])
            us = u[mp:].reshape(n_sample, l_sample, d)
            tails_p.append(jnp.concatenate([jnp.zeros((n_prompt, keep, d), F32), up], axis=1)[:, -keep:])
            tails_s.append(jnp.concatenate([states[jj], us], axis=1)[:, -keep:])
        return jnp.stack(tails_p), jnp.stack(tails_s)

    sc_p, sc_s = tails(sc_us, state_sconv, kw_sc - 1)
    cf_p, cf_s = tails(cf_us, state_cconv, kw_cf - 1)
    mem_k_p = jnp.stack(kps).reshape(depth, n_prompt, mem_len, mem_heads, mem_hd)
    mem_v_p = jnp.stack(vps).reshape(depth, n_prompt, mem_len, mem_heads, mem_hd)
    return (y_prompt, y_sample, gla_p, gla_s, sc_p, sc_s, cf_p, cf_s, mem_k_p, mem_v_p)
```

```python
import functools

import jax
import jax.numpy as jnp
from jax import lax
from jax.experimental import pallas as pl
from jax.experimental.pallas import tpu as pltpu

F32 = jnp.float32
BF16 = jnp.bfloat16

EPS = 1e-6
GLA_TAU = 16.0
GLA_SUB = 16
GLA_CHUNK = 128
GLA_PROMPT_HEADS = 2
V7X_VMEM_LIMIT_BYTES = 56 * 1024 * 1024
LANES = 128
MM_ROWS = 1024
WIDE_ROWS = 1536
ROW_TILES = (512, 256, 128, 64, 32, 16, 8)
SCONV_ROWS = 512
CCONV_ROWS = 128
ATTN_ROWS = 512


def _params(n_axes):
    return pltpu.CompilerParams(
        dimension_semantics=("arbitrary",) * n_axes,
        vmem_limit_bytes=V7X_VMEM_LIMIT_BYTES,
    )


def _dot(a, b):
    return jnp.dot(a, b, preferred_element_type=F32)


def _dot_nt(a, b):
    return lax.dot_general(a, b, (((1,), (1,)), ((), ())), preferred_element_type=F32)


def _dot_tn(a, b):
    return lax.dot_general(a, b, (((0,), (0,)), ((), ())), preferred_element_type=F32)


def _sigmoid(x):
    return 1.0 / (1.0 + jnp.exp(-x))


def _pick(n, prefs):
    for p in prefs:
        if p <= n and n % p == 0:
            return p
    return n


def _mm_body(*refs, groups, nk, kind, has_ss, has_bias, has_side, inv_d):
    it = iter(refs)
    a_ref = next(it)
    w_refs = [next(it) for _ in range(groups)]
    ss_ref = next(it) if has_ss else None
    b_refs = [next(it) for _ in range(groups)] if has_bias else None
    if kind == "delta":
        res_ref = next(it)
        gain_ref = next(it)
    side_in_ref = next(it) if has_side else None
    n_out = {"plain": 1, "relu2": 1, "glu": 1, "sconv": 2, "delta": 3}[kind]
    out_refs = [next(it) for _ in range(n_out)]
    side_out_ref = next(it) if has_side else None
    acc_refs = [next(it) for _ in range(groups)] if nk > 1 and kind != "delta" else None

    j = pl.program_id(1)
    k = pl.program_id(2)
    if has_side:
        side_out_ref[...] = side_in_ref[...].astype(BF16)

    def products():
        a = a_ref[...]
        if a.dtype != BF16:
            a = a.astype(BF16)
        return [_dot(a, w_ref[...].astype(BF16)) for w_ref in w_refs]

    def epilogue(accs):
        if has_ss:
            r = lax.rsqrt(ss_ref[...] * inv_d + EPS)
            accs = [x * r for x in accs]
        if has_bias:
            accs = [x + b_ref[...] for x, b_ref in zip(accs, b_refs)]
        if kind == "plain":
            out_refs[0][...] = accs[0].astype(out_refs[0].dtype)
        elif kind == "relu2":
            h = jnp.maximum(accs[0], 0.0)
            out_refs[0][...] = (h * h).astype(out_refs[0].dtype)
        elif kind == "glu":
            out_refs[0][...] = accs[0] * _sigmoid(accs[1])
        else:
            assert kind == "sconv"
            out_refs[0][...] = accs[0]
            out_refs[1][...] = accs[1] * accs[2]

    def delta_tail(xn):
        out_refs[1][...] = (xn * gain_ref[...]).astype(BF16)
        s = jnp.sum(xn * xn, axis=1, keepdims=True)

        @pl.when(j == 0)
        def _():
            out_refs[2][...] = s

        @pl.when(j > 0)
        def _():
            out_refs[2][...] += s

    if nk == 1 and kind == "delta":
        xn = products()[0]
        if has_bias:
            xn = xn + b_refs[0][...]
        xn = res_ref[...] + xn
        out_refs[0][...] = xn
        delta_tail(xn)
    elif nk == 1:
        epilogue(products())
    elif kind == "delta":
        @pl.when(k == 0)
        def _():
            seed = res_ref[...]
            if has_bias:
                seed = seed + b_refs[0][...]
            out_refs[0][...] = seed

        out_refs[0][...] = out_refs[0][...] + products()[0]

        @pl.when(k == nk - 1)
        def _():
            delta_tail(out_refs[0][...])
    else:
        @pl.when(k == 0)
        def _():
            for acc_ref in acc_refs:
                acc_ref[...] = jnp.zeros_like(acc_ref)

        for acc_ref, p in zip(acc_refs, products()):
            acc_ref[...] = acc_ref[...] + p

        @pl.when(k == nk - 1)
        def _():
            epilogue([acc_ref[...] for acc_ref in acc_refs])


def _matmul(a, w, layer, *, kind, name, groups=1, ss=None, bias=None, bias_layer=0,
            res=None, gain=None, gain_layer=0, out_dtype=F32, tm=None, tn=512, tk=None,
            side_w=None, side_layer=0):
    m, kdim = a.shape
    n = w.shape[-1] // groups
    tm = _pick(m, (MM_ROWS if tm is None else tm,) + ROW_TILES)
    tn = _pick(n, (tn, 512, 256, 128))
    tk = kdim if tk is None else _pick(kdim, (tk, 1024, 512, 256, 128))
    nk = kdim // tk
    nj = n // tn
    grid = (m // tm, nj, nk)
    has_ss = ss is not None
    has_bias = bias is not None

    in_specs = [pl.BlockSpec((tm, tk), lambda i, j, k: (i, k))]
    operands = [a]
    for g in range(groups):
        in_specs.append(pl.BlockSpec((None, tk, tn), lambda i, j, k, g=g: (layer, k, g * nj + j)))
        operands.append(w)
    if has_ss:
        in_specs.append(pl.BlockSpec((tm, 1), lambda i, j, k: (i, 0)))
        operands.append(ss)
    if has_bias:
        b3 = bias.reshape(bias.shape[0], 1, bias.shape[-1])
        for g in range(groups):
            in_specs.append(pl.BlockSpec((None, 1, tn), lambda i, j, k, g=g: (bias_layer, 0, g * nj + j)))
            operands.append(b3)
    tile = pl.BlockSpec((tm, tn), lambda i, j, k: (i, j))
    if kind == "delta":
        g3 = gain.reshape(-1, 1, gain.shape[-1])
        in_specs += [tile, pl.BlockSpec((None, 1, tn), lambda i, j, k: (gain_layer, 0, j))]
        operands += [res, g3]
        out_shape = [jax.ShapeDtypeStruct((m, n), F32), jax.ShapeDtypeStruct((m, n), BF16),
                     jax.ShapeDtypeStruct((m, 1), F32)]
        out_specs = [tile, tile, pl.BlockSpec((tm, 1), lambda i, j, k: (i, 0))]
    elif kind == "sconv":
        out_shape = [jax.ShapeDtypeStruct((m, n), F32)] * 2
        out_specs = [tile, tile]
    else:
        out_shape = [jax.ShapeDtypeStruct((m, n), out_dtype)]
        out_specs = [tile]
    scratch = [pltpu.VMEM((tm, tn), F32) for _ in range(groups)] if nk > 1 and kind != "delta" else []

    has_side = side_w is not None
    if has_side:
        _, k2, n2 = side_w.shape
        steps = grid[0] * nj * nk
        sr = min([r for r in ROW_TILES if k2 % r == 0 and k2 // r <= steps] or [k2])
        last = k2 // sr - 1
        in_specs.append(pl.BlockSpec(
            (None, sr, n2), lambda i, j, k: (side_layer, jnp.minimum((i * nj + j) * nk + k, last), 0)))
        operands.append(side_w)
        out_shape = out_shape + [jax.ShapeDtypeStruct((k2, n2), BF16)]
        out_specs = out_specs + [pl.BlockSpec(
            (sr, n2), lambda i, j, k: (jnp.minimum((i * nj + j) * nk + k, last), 0))]

    body = functools.partial(_mm_body, groups=groups, nk=nk, kind=kind, has_ss=has_ss,
                             has_bias=has_bias, has_side=has_side, inv_d=1.0 / kdim)
    outs = pl.pallas_call(
        body, out_shape=out_shape, grid=grid, in_specs=in_specs, out_specs=out_specs,
        scratch_shapes=scratch, compiler_params=_params(3), name=name,
    )(*operands)
    return outs if len(outs) > 1 else outs[0]


def _prep_body(xp_ref, xs_ref, g_ref, x_ref, xb_ref, ss_ref, *, n_prompt_tiles):
    i = pl.program_id(0)

    def emit(x):
        x_ref[...] = x
        xb_ref[...] = (x * g_ref[...]).astype(BF16)
        ss_ref[...] = jnp.sum(x * x, axis=1, keepdims=True)

    @pl.when(i < n_prompt_tiles)
    def _():
        emit(xp_ref[...])

    @pl.when(i >= n_prompt_tiles)
    def _():
        emit(xs_ref[...])


def _prep(xp2, xs2, gain, gain_layer):
    mp, d = xp2.shape
    ms = xs2.shape[0]
    tr = _pick(ms, (256, 128, 64, 32, 16, 8))
    npt, nst = mp // tr, ms // tr
    m = mp + ms
    g3 = gain.reshape(-1, 1, d)
    row = pl.BlockSpec((tr, d), lambda i: (i, 0))
    return pl.pallas_call(
        functools.partial(_prep_body, n_prompt_tiles=npt),
        out_shape=[jax.ShapeDtypeStruct((m, d), F32), jax.ShapeDtypeStruct((m, d), BF16),
                   jax.ShapeDtypeStruct((m, 1), F32)],
        grid=(npt + nst,),
        in_specs=[pl.BlockSpec((tr, d), lambda i: (jnp.minimum(i, npt - 1), 0)),
                  pl.BlockSpec((tr, d), lambda i: (jnp.maximum(i - npt, 0), 0)),
                  pl.BlockSpec((None, 1, d), lambda i: (gain_layer, 0, 0))],
        out_specs=[row, row, pl.BlockSpec((tr, 1), lambda i: (i, 0))],
        compiler_params=_params(1), name="prep",
    )(xp2, xs2, g3)


def _final_body(x_ref, ss_ref, g_ref, y_ref, *, inv_d):
    y_ref[...] = x_ref[...] * lax.rsqrt(ss_ref[...] * inv_d + EPS) * g_ref[...]


def _final_norm(x, ss, gain, row0, rows, name):
    d = x.shape[1]
    tr = _pick(rows, (256, 128, 64, 32, 16, 8))
    off = row0 // tr
    return pl.pallas_call(
        functools.partial(_final_body, inv_d=1.0 / d),
        out_shape=jax.ShapeDtypeStruct((rows, d), F32),
        grid=(rows // tr,),
        in_specs=[pl.BlockSpec((tr, d), lambda i: (off + i, 0)),
                  pl.BlockSpec((tr, 1), lambda i: (off + i, 0)),
                  pl.BlockSpec((1, d), lambda i: (0, 0))],
        out_specs=pl.BlockSpec((tr, d), lambda i: (i, 0)),
        compiler_params=_params(1), name=name,
    )(x, ss, gain.reshape(1, d))


def _log_sigmoid(x):
    return jnp.minimum(x, 0.0) - jnp.log1p(jnp.exp(-jnp.abs(x)))


def _split3(x):
    hi = x.astype(BF16)
    r1 = x - hi.astype(F32)
    mid = r1.astype(BF16)
    lo = (r1 - mid.astype(F32)).astype(BF16)
    return hi, mid, lo


def _gla_log_decay(g1, wg2, bg):
    return _log_sigmoid(_dot(g1.astype(BF16), wg2.astype(BF16)) + bg) * (1.0 / GLA_TAU)


def _gla_chunk(q, k, v, g, s, *, sub, scale):
    c, hk = q.shape

    row = lax.broadcasted_iota(jnp.int32, (c, c), 0)
    col = lax.broadcasted_iota(jnp.int32, (c, c), 1)
    causal = col <= row
    tril = jnp.where(causal, 1.0, 0.0).astype(BF16)
    g_parts = _split3(g)
    b = _dot(tril, g_parts[0]) + _dot(tril, g_parts[1]) + _dot(tril, g_parts[2])
    ones = jnp.ones((c, LANES), BF16)
    b_last_col = (_dot_tn(g_parts[0], ones) + _dot_tn(g_parts[1], ones)
                  + _dot_tn(g_parts[2], ones))[:, :1]
    b_last = b[c - 1:c, :]

    qs = q * scale
    vb = v.astype(BF16)
    o = _dot((qs * jnp.exp(b)).astype(BF16), s.astype(BF16))

    key_pos = lax.broadcasted_iota(jnp.int32, (c, 1), 0)
    att_rows = []
    for i in range(c // sub):
        lo, hi = i * sub, (i + 1) * sub
        b_ref = b[lo - 1:lo, :] if i > 0 else jnp.zeros((1, hk), F32)
        qi = (qs[lo:hi] * jnp.exp(b[lo:hi] - b_ref)).astype(BF16)
        expo = jnp.where(key_pos < hi, b_ref - b, -1e30)
        kk = (k * jnp.exp(expo)).astype(BF16)
        att_rows.append(_dot_nt(qi, kk))
    att = att_rows[0] if len(att_rows) == 1 else jnp.concatenate(att_rows, axis=0)
    att = jnp.where(causal, att, 0.0)
    o = o + _dot(att.astype(BF16), vb)

    kd = (k * jnp.exp(b_last - b)).astype(BF16)
    s_new = jnp.exp(b_last_col) * s + _dot_tn(kd, vb)
    return o, s_new


def _gla_gate_out(o, rg, ng):
    on = o * lax.rsqrt(jnp.mean(o * o, axis=-1, keepdims=True) + EPS) * ng
    return (on * (rg * _sigmoid(rg))).astype(BF16)


def _gla_prompt_body(q_ref, k_ref, v_ref, g1_ref, wg2_ref, bg_ref, rg_ref, ng_ref,
                     og_ref, sout_ref, s_ref, *, nc, scale):
    c = pl.program_id(2)

    @pl.when(c == 0)
    def _():
        s_ref[...] = jnp.zeros_like(s_ref)

    g = _gla_log_decay(g1_ref[...], wg2_ref[...], bg_ref[...])
    hk, hv = s_ref.shape[1], s_ref.shape[2]
    for h in range(s_ref.shape[0]):
        ck = slice(h * hk, (h + 1) * hk)
        cv = slice(h * hv, (h + 1) * hv)
        o, s_new = _gla_chunk(q_ref[:, ck], k_ref[:, ck], v_ref[:, cv], g[:, ck], s_ref[h],
                              sub=GLA_SUB, scale=scale)
        s_ref[h] = s_new
        og_ref[:, cv] = _gla_gate_out(o, rg_ref[:, cv], ng_ref[:, cv])

    @pl.when(c == nc - 1)
    def _():
        sout_ref[...] = s_ref[...]


def _gla_sample_body(q_ref, k_ref, v_ref, g1_ref, wg2_ref, bg_ref, rg_ref, ng_ref, s0_ref, *rest,
                     scale, sub, nh):
    og_ref, sout_ref = rest[-2], rest[-1]
    hk, hv = s0_ref.shape[1], s0_ref.shape[2]
    g = _gla_log_decay(g1_ref[...], wg2_ref[...], bg_ref[...])
    for h in range(nh):
        ck = slice(h * hk, (h + 1) * hk)
        cv = slice(h * hv, (h + 1) * hv)
        o, s_new = _gla_chunk(q_ref[:, ck], k_ref[:, ck], v_ref[:, cv], g[:, ck], s0_ref[h],
                              sub=sub, scale=scale)
        og_ref[:, cv] = _gla_gate_out(o, rg_ref[:, cv], ng_ref[:, cv])
        sout_ref[h] = s_new


def _gla_scan(q, k, v, g1, rg, wg2p, bg, ng, state_in, layer_j, sp_prev, gs_prev, *,
              n_prompt, l_prompt, n_sample, l_sample):
    n_gla, _, nh, hk, hv = state_in.shape
    m = q.shape[0]
    scale = float(hk) ** -0.5
    rk = g1.shape[1]
    wg3 = wg2p
    bg3 = bg.reshape(n_gla, 1, nh * hk)
    ng3 = ng.reshape(n_gla, 1, nh * hv)

    cc = _pick(l_prompt, (GLA_CHUNK, 32, 16, 8))
    nc = l_prompt // cc
    sp_shape = jax.ShapeDtypeStruct((n_gla, n_prompt, nh, hk, hv), F32)
    hps = _pick(nh, (GLA_PROMPT_HEADS, 1))
    rowsel = lambda b, h, c: (b * nc + c, h)
    in_specs = [pl.BlockSpec((cc, hps * hk), rowsel), pl.BlockSpec((cc, hps * hk), rowsel),
                pl.BlockSpec((cc, hps * hv), rowsel),
                pl.BlockSpec((cc, rk), lambda b, h, c: (b * nc + c, 0)),
                pl.BlockSpec((None, rk, hps * hk), lambda b, h, c: (layer_j, 0, h)),
                pl.BlockSpec((None, 1, hps * hk), lambda b, h, c: (layer_j, 0, h)),
                pl.BlockSpec((cc, hps * hv), rowsel),
                pl.BlockSpec((None, 1, hps * hv), lambda b, h, c: (layer_j, 0, h))]
    operands = [q, k, v, g1, wg3, bg3, rg, ng3]
    aliases = {}
    if sp_prev is not None:
        in_specs.append(pl.BlockSpec(memory_space=pl.ANY))
        operands.append(sp_prev)
        aliases = {len(operands) - 1: 1}

    def prompt_body(*refs):
        if sp_prev is not None:
            refs = refs[:8] + refs[9:]
        _gla_prompt_body(*refs, nc=nc, scale=scale)

    og, sp = pl.pallas_call(
        prompt_body,
        out_shape=[jax.ShapeDtypeStruct((m, nh * hv), BF16), sp_shape],
        grid=(n_prompt, nh // hps, nc), in_specs=in_specs,
        out_specs=[pl.BlockSpec((cc, hps * hv), rowsel),
                   pl.BlockSpec((None, None, hps, hk, hv), lambda b, h, c: (layer_j, b, h, 0, 0))],
        scratch_shapes=[pltpu.VMEM((hps, hk, hv), F32)],
        input_output_aliases=aliases, compiler_params=_params(3), name="gla_prompt",
    )(*operands)

    r0 = (n_prompt * l_prompt) // l_sample
    rows_s = lambda b: (r0 + b, 0)
    wsel = lambda b: (layer_j, 0, 0)
    state_spec = pl.BlockSpec((None, None, nh, hk, hv), lambda b: (layer_j, b, 0, 0, 0))
    in_specs = [pl.BlockSpec((l_sample, nh * hk), rows_s), pl.BlockSpec((l_sample, nh * hk), rows_s),
                pl.BlockSpec((l_sample, nh * hv), rows_s),
                pl.BlockSpec((l_sample, rk), rows_s),
                pl.BlockSpec((None, rk, nh * hk), wsel),
                pl.BlockSpec((None, 1, nh * hk), wsel),
                pl.BlockSpec((l_sample, nh * hv), rows_s),
                pl.BlockSpec((None, 1, nh * hv), wsel),
                state_spec,
                pl.BlockSpec(memory_space=pl.ANY)]
    operands = [q, k, v, g1, wg3, bg3, rg, ng3, state_in, og]
    aliases = {9: 0}
    if gs_prev is not None:
        in_specs.append(pl.BlockSpec(memory_space=pl.ANY))
        operands.append(gs_prev)
        aliases[10] = 1
    og, gs_new = pl.pallas_call(
        functools.partial(_gla_sample_body, scale=scale, sub=min(GLA_SUB, l_sample), nh=nh),
        out_shape=[jax.ShapeDtypeStruct((m, nh * hv), BF16),
                   jax.ShapeDtypeStruct((n_gla, n_sample, nh, hk, hv), F32)],
        grid=(n_sample,), in_specs=in_specs,
        out_specs=[pl.BlockSpec((l_sample, nh * hv), rows_s), state_spec],
        input_output_aliases=aliases, compiler_params=_params(1), name="gla_sample",
    )(*operands)
    return og, sp, gs_new


def _softmax_rows(s):
    e = jnp.exp(s - jnp.max(s, axis=-1, keepdims=True))
    return e / jnp.sum(e, axis=-1, keepdims=True)


def _attn_prompt_body(q_ref, k_ref, v_ref, o_ref, *, scale, nh, hd):
    for h in range(nh):
        cols = slice(h * hd, (h + 1) * hd)
        q = q_ref[:, cols].astype(BF16)
        p = _softmax_rows(_dot_nt(q, k_ref[:, cols].astype(BF16)) * scale)
        o_ref[:, cols] = _dot(p.astype(BF16), v_ref[:, cols].astype(BF16)).astype(BF16)


def _attn_sample_body(q_ref, k_ref, v_ref, o_in_ref, o_ref, *, scale, bb, ls):
    del o_in_ref
    nh, hd = k_ref.shape[2], k_ref.shape[3]
    kh = pltpu.einshape("bmhd->hbmd", k_ref[...])
    vh = pltpu.einshape("bmhd->hbmd", v_ref[...])
    for h in range(nh):
        cols = slice(h * hd, (h + 1) * hd)
        q = q_ref[:, cols].reshape(bb, ls, hd).astype(BF16)
        s = lax.dot_general(q, kh[h].astype(BF16), (((2,), (2,)), ((0,), (0,))),
                            preferred_element_type=F32) * scale
        p = _softmax_rows(s).astype(BF16)
        o = lax.dot_general(p, vh[h].astype(BF16), (((2,), (1,)), ((0,), (0,))),
                            preferred_element_type=F32)
        o_ref[:, cols] = o.reshape(bb * ls, hd).astype(BF16)


def _mem_attention(q, kp, vp, cache_k, cache_v, layer, *, n_prompt, l_prompt, n_sample, l_sample):
    m, dm = q.shape
    _, _, mem_len, nh, hd = cache_k.shape
    scale = float(hd) ** -0.5
    tq = _pick(l_prompt, (ATTN_ROWS,) + ROW_TILES)
    nt = l_prompt // tq
    o = pl.pallas_call(
        functools.partial(_attn_prompt_body, scale=scale, nh=nh, hd=hd),
        out_shape=jax.ShapeDtypeStruct((m, dm), BF16),
        grid=(n_prompt, nt),
        in_specs=[pl.BlockSpec((tq, dm), lambda b, t: (b * nt + t, 0)),
                  pl.BlockSpec((mem_len, dm), lambda b, t: (b, 0)),
                  pl.BlockSpec((mem_len, dm), lambda b, t: (b, 0))],
        out_specs=pl.BlockSpec((tq, dm), lambda b, t: (b * nt + t, 0)),
        compiler_params=_params(2), name="attn_prompt",
    )(q, kp, vp)

    bb = _pick(n_sample, (8, 4, 2, 1))
    r0 = (n_prompt * l_prompt) // (bb * l_sample)
    cache_spec = pl.BlockSpec((None, bb, mem_len, nh, hd), lambda i: (layer, i, 0, 0, 0))
    o = pl.pallas_call(
        functools.partial(_attn_sample_body, scale=scale, bb=bb, ls=l_sample),
        out_shape=jax.ShapeDtypeStruct((m, dm), BF16),
        grid=(n_sample // bb,),
        in_specs=[pl.BlockSpec((bb * l_sample, dm), lambda i: (r0 + i, 0)),
                  cache_spec, cache_spec,
                  pl.BlockSpec(memory_space=pl.ANY)],
        out_specs=pl.BlockSpec((bb * l_sample, dm), lambda i: (r0 + i, 0)),
        input_output_aliases={3: 0}, compiler_params=_params(1), name="attn_sample",
    )(q, cache_k, cache_v, o)
    return o


def _shift_rows(u, hist_rows, t, axis):
    n = len(hist_rows)
    out = []
    for s in range(1, n + 1):
        shifted = pltpu.roll(u, s, axis=axis)
        for tt in range(s):
            shifted = jnp.where(t == tt, hist_rows[n - s + tt], shifted)
        out.append(shifted)
    return out


def _sconv_prompt_body(u_ref, prev_ref, bg_ref, w_ref, z_ref, *, tiles_per_seq):
    i = pl.program_id(0)
    u = u_ref[...]
    keep = jnp.where(i % tiles_per_seq == 0, 0.0, 1.0)
    prev = prev_ref[...] * keep
    kw = w_ref.shape[0]
    hist = [prev[8 - (kw - 1) + r: 8 - (kw - 1) + r + 1, :] for r in range(kw - 1)]
    t = lax.broadcasted_iota(jnp.int32, (u.shape[0], 1), 0)
    shifted = _shift_rows(u, hist, t, 0)
    y = w_ref[kw - 1:kw, :] * u
    for s, us in enumerate(shifted, start=1):
        y = y + w_ref[kw - 1 - s:kw - s, :] * us
    z_ref[...] = (bg_ref[...] * y).astype(BF16)


def _sconv_sample_body(u_ref, hist_ref, bg_ref, w_ref, z_in_ref, z_ref, *, bb, ls):
    del z_in_ref
    tc = u_ref.shape[1]
    u = u_ref[...].reshape(bb, ls, tc)
    kw = w_ref.shape[0]
    hist = [hist_ref[:, r:r + 1, :] for r in range(kw - 1)]
    t = lax.broadcasted_iota(jnp.int32, (1, ls, 1), 1)
    shifted = _shift_rows(u, hist, t, 1)
    y = w_ref[kw - 1:kw, :] * u
    for s, us in enumerate(shifted, start=1):
        y = y + w_ref[kw - 1 - s:kw - s, :] * us
    z_ref[...] = (bg_ref[...] * y.reshape(bb * ls, tc)).astype(BF16)


def _short_conv(u, bgate, wconv, layer_j, state, *, n_prompt, l_prompt, n_sample, l_sample):
    m, d = u.shape
    kw = wconv.shape[1]
    tr = _pick(l_prompt, (SCONV_ROWS,) + ROW_TILES)
    tc = _pick(d, (512, 256, 128))
    tps = l_prompt // tr
    z = pl.pallas_call(
        functools.partial(_sconv_prompt_body, tiles_per_seq=tps),
        out_shape=jax.ShapeDtypeStruct((m, d), BF16),
        grid=(n_prompt * tps, d // tc),
        in_specs=[pl.BlockSpec((tr, tc), lambda i, j: (i, j)),
                  pl.BlockSpec((8, tc), lambda i, j: (jnp.maximum(i * (tr // 8) - 1, 0), j)),
                  pl.BlockSpec((tr, tc), lambda i, j: (i, j)),
                  pl.BlockSpec((None, kw, tc), lambda i, j: (layer_j, 0, j))],
        out_specs=pl.BlockSpec((tr, tc), lambda i, j: (i, j)),
        compiler_params=_params(2), name="sconv_prompt",
    )(u, u, bgate, wconv)

    bb = _pick(n_sample, (64, 32, 16, 8, 4, 2, 1))
    r0 = (n_prompt * l_prompt) // (bb * l_sample)
    z = pl.pallas_call(
        functools.partial(_sconv_sample_body, bb=bb, ls=l_sample),
        out_shape=jax.ShapeDtypeStruct((m, d), BF16),
        grid=(n_sample // bb, d // tc),
        in_specs=[pl.BlockSpec((bb * l_sample, tc), lambda i, j: (r0 + i, j)),
                  pl.BlockSpec((None, bb, kw - 1, tc), lambda i, j: (layer_j, i, 0, j)),
                  pl.BlockSpec((bb * l_sample, tc), lambda i, j: (r0 + i, j)),
                  pl.BlockSpec((None, kw, tc), lambda i, j: (layer_j, 0, j)),
                  pl.BlockSpec(memory_space=pl.ANY)],
        out_specs=pl.BlockSpec((bb * l_sample, tc), lambda i, j: (r0 + i, j)),
        input_output_aliases={4: 0}, compiler_params=_params(2), name="sconv_sample",
    )(u, state, bgate, wconv, z)
    return z


def _cconv_core(u3, hist3, w_ref, bdw_ref, lg_ref, lb_ref, ext_ref, shift_ref, z_ref, out_ref, *, tc):
    j = pl.program_id(1)
    nj = pl.num_programs(1)
    bb, ll, _ = u3.shape
    hb = hist3.shape[1]
    kw = w_ref.shape[0]
    off = (-hb) % 8
    ext_ref[:, off:off + hb, :] = hist3
    ext_ref[:, off + hb:off + hb + ll, :] = u3
    base = off + hb - (kw - 1)
    acc = jnp.zeros((bb, ll, tc), F32) + bdw_ref[...]
    for r in range(min(8, kw)):
        n_a = (kw - 1 - r) // 8 + 1
        rows = ll + 8 * (n_a - 1)
        shift_ref[:, 0:rows, :] = ext_ref[:, base + r:base + r + rows, :]
        for a in range(n_a):
            kk = 8 * a + r
            acc = acc + w_ref[kk:kk + 1, :] * shift_ref[:, 8 * a:8 * a + ll, :]
    z_ref[j] = acc

    @pl.when(j == nj - 1)
    def _():
        n_tiles = z_ref.shape[0]
        inv_d = 1.0 / (n_tiles * tc)
        z = z_ref[...]
        mu = jnp.sum(jnp.sum(z, axis=3, keepdims=True), axis=0, keepdims=True) * inv_d
        zc = z - mu
        var = jnp.sum(jnp.sum(zc * zc, axis=3, keepdims=True), axis=0, keepdims=True) * inv_d
        zn = zc * lax.rsqrt(var + EPS)
        for jj in range(n_tiles):
            cols = slice(jj * tc, (jj + 1) * tc)
            y = zn[jj] * lg_ref[:, cols] + lb_ref[:, cols]
            y = y * _sigmoid(y)
            out_ref[:, cols] = y.reshape(bb * ll, tc).astype(BF16)


def _cconv_prompt_body(u_ref, halo_ref, w_ref, bdw_ref, lg_ref, lb_ref, out_ref, ext_ref, shift_ref,
                       z_ref, *, tiles_per_seq, tc):
    i = pl.program_id(0)
    keep = jnp.where(i % tiles_per_seq == 0, 0.0, 1.0)
    u3 = u_ref[...][None]
    hist3 = (halo_ref[...] * keep)[None]
    _cconv_core(u3, hist3, w_ref, bdw_ref, lg_ref, lb_ref, ext_ref, shift_ref, z_ref, out_ref, tc=tc)


def _cconv_sample_body(u_ref, hist_ref, w_ref, bdw_ref, lg_ref, lb_ref, o_in_ref, out_ref, ext_ref,
                       shift_ref, z_ref, *, bb, ls, tc):
    del o_in_ref
    u3 = u_ref[...].reshape(bb, ls, tc)
    _cconv_core(u3, hist_ref[...], w_ref, bdw_ref, lg_ref, lb_ref, ext_ref, shift_ref, z_ref, out_ref,
                tc=tc)


def _conformer_conv(u, wdw, bdw, lng, lnb, layer_j, state, *, n_prompt, l_prompt, n_sample, l_sample):
    m, d = u.shape
    kw = wdw.shape[1]
    halo = 8 * ((kw - 1 + 7) // 8)
    shift_pad = 8 * ((kw - 1) // 8)
    tr = _pick(l_prompt, (CCONV_ROWS, 64, 32))
    tc = _pick(d, (512, 256, 128))
    tps = l_prompt // tr
    b3 = bdw.reshape(-1, 1, d)
    g3 = lng.reshape(-1, 1, d)
    l3 = lnb.reshape(-1, 1, d)
    vec = lambda i, j: (layer_j, 0, j)
    full = lambda i, j: (layer_j, 0, 0)
    zc = pl.pallas_call(
        functools.partial(_cconv_prompt_body, tiles_per_seq=tps, tc=tc),
        out_shape=jax.ShapeDtypeStruct((m, d), BF16),
        grid=(n_prompt * tps, d // tc),
        in_specs=[pl.BlockSpec((tr, tc), lambda i, j: (i, j)),
                  pl.BlockSpec((halo, tc), lambda i, j: (jnp.maximum(i * (tr // halo) - 1, 0), j)),
                  pl.BlockSpec((None, kw, tc), vec), pl.BlockSpec((None, 1, tc), vec),
                  pl.BlockSpec((None, 1, d), full), pl.BlockSpec((None, 1, d), full)],
        out_specs=pl.BlockSpec((tr, d), lambda i, j: (i, 0)),
        scratch_shapes=[pltpu.VMEM((1, tr + halo, tc), F32), pltpu.VMEM((1, tr + shift_pad, tc), F32),
                        pltpu.VMEM((d // tc, 1, tr, tc), F32)],
        compiler_params=_params(2), name="cconv_prompt",
    )(u, u, wdw, b3, g3, l3)

    bb = _pick(n_sample, (8, 4, 2, 1))
    r0 = (n_prompt * l_prompt) // (bb * l_sample)
    hb = kw - 1
    ext_rows = 8 * (((-hb) % 8 + hb + l_sample + 7) // 8)
    zc = pl.pallas_call(
        functools.partial(_cconv_sample_body, bb=bb, ls=l_sample, tc=tc),
        out_shape=jax.ShapeDtypeStruct((m, d), BF16),
        grid=(n_sample // bb, d // tc),
        in_specs=[pl.BlockSpec((bb * l_sample, tc), lambda i, j: (r0 + i, j)),
                  pl.BlockSpec((None, bb, hb, tc), lambda i, j: (layer_j, i, 0, j)),
                  pl.BlockSpec((None, kw, tc), vec), pl.BlockSpec((None, 1, tc), vec),
                  pl.BlockSpec((None, 1, d), full), pl.BlockSpec((None, 1, d), full),
                  pl.BlockSpec(memory_space=pl.ANY)],
        out_specs=pl.BlockSpec((bb * l_sample, d), lambda i, j: (r0 + i, 0)),
        scratch_shapes=[pltpu.VMEM((bb, ext_rows, tc), F32),
                        pltpu.VMEM((bb, l_sample + shift_pad, tc), F32),
                        pltpu.VMEM((d // tc, bb, l_sample, tc), F32)],
        input_output_aliases={6: 0}, compiler_params=_params(2), name="cconv_sample",
    )(u, state, wdw, b3, g3, l3, zc)
    return zc


def kernel(x_prompt, x_sample, state_gla, state_sconv, state_cconv, cache_mem_k, cache_mem_v, mem_prompt, norm_mix, norm_mem, norm_ffn, norm_final, gla_wq, gla_wk, gla_wv, gla_wg1, gla_wg2, gla_bg, gla_wr, gla_norm, gla_wo, sc_win, sc_wconv, sc_wout, cf_w1, cf_b1, cf_wdw, cf_bdw, cf_ln_g, cf_ln_b, cf_w2, cf_b2, mem_wq, mem_wk, mem_wv, mem_wo, ffn_up, ffn_down):
    n_prompt, l_prompt, d = x_prompt.shape
    n_sample, l_sample, _ = x_sample.shape
    depth = norm_mix.shape[0]
    n_mixers = 3
    mp, ms = n_prompt * l_prompt, n_sample * l_sample
    seq = dict(n_prompt=n_prompt, l_prompt=l_prompt, n_sample=n_sample, l_sample=l_sample)
    _, _, mem_len, mem_heads, mem_hd = cache_mem_k.shape
    mem_dim = mem_heads * mem_hd
    n_gla, _, gla_heads, hk, hv = state_gla.shape
    kw_sc = sc_wconv.shape[1]
    kw_cf = cf_wdw.shape[1]

    mem2 = mem_prompt.reshape(n_prompt * mem_len, d)
    kps = [_matmul(mem2, mem_wk, i, kind="plain", name="mem_k", tm=512) for i in range(depth)]
    vps = [_matmul(mem2, mem_wv, i, kind="plain", name="mem_v", tm=512) for i in range(depth)]

    rank = gla_wg1.shape[-1]
    rpad = LANES * ((rank + LANES - 1) // LANES)
    wg1p = jnp.pad(gla_wg1, ((0, 0), (0, 0), (0, rpad - rank)))
    wg2p = jnp.pad(gla_wg2, ((0, 0), (0, rpad - rank), (0, 0)))

    x, xb, ss = _prep(x_prompt.reshape(mp, d), x_sample.reshape(ms, d), norm_mix, 0)

    gla_p = gla_s = None
    sc_us, cf_us = [], []
    w_up = None
    for i in range(depth):
        kind, j = i % n_mixers, i // n_mixers
        if kind == 0:
            q = _matmul(xb, gla_wq, j, kind="plain", name="gla_q", ss=ss, tm=WIDE_ROWS)
            k = _matmul(xb, gla_wk, j, kind="plain", name="gla_k", ss=ss, tm=WIDE_ROWS)
            v = _matmul(xb, gla_wv, j, kind="plain", name="gla_v", ss=ss, tm=WIDE_ROWS)
            rg = _matmul(xb, gla_wr, j, kind="plain", name="gla_r", ss=ss, tm=WIDE_ROWS)
            g1 = _matmul(xb, wg1p, j, kind="plain", name="gla_g1", ss=ss, tn=LANES)
            mix, gla_p, gla_s = _gla_scan(q, k, v, g1, rg, wg2p, gla_bg, gla_norm, state_gla, j,
                                          gla_p, gla_s, **seq)
            x, xb, ss = _matmul(mix, gla_wo, j, kind="delta", name="gla_o", res=x,
                                gain=norm_mem, gain_layer=i)
        elif kind == 1:
            bgate, u = _matmul(xb, sc_win, j, kind="sconv", name="sc_in", groups=3, ss=ss, tn=256)
            sc_us.append(u)
            z = _short_conv(u, bgate, sc_wconv, j, state_sconv, **seq)
            x, xb, ss = _matmul(z, sc_wout, j, kind="delta", name="sc_out", res=x,
                                gain=norm_mem, gain_layer=i)
        else:
            u = _matmul(xb, cf_w1, j, kind="glu", name="cf_in", groups=2, ss=ss, bias=cf_b1,
                        bias_layer=j, tm=WIDE_ROWS, tn=256)
            cf_us.append(u)
            zc = _conformer_conv(u, cf_wdw, cf_bdw, cf_ln_g, cf_ln_b, j, state_cconv, **seq)
            x, xb, ss = _matmul(zc, cf_w2, j, kind="delta", name="cf_out", res=x, bias=cf_b2,
                                bias_layer=j, gain=norm_mem, gain_layer=i)

        q = _matmul(xb, mem_wq, i, kind="plain", name="mem_q", ss=ss)
        o = _mem_attention(q, kps[i], vps[i], cache_mem_k, cache_mem_v, i, **seq)
        x, xb, ss = _matmul(o, mem_wo, i, kind="delta", name="mem_o", res=x, gain=norm_ffn,
                            gain_layer=i, tn=1024)

        if w_up is None:
            hid, w_down = _matmul(xb, ffn_up, i, kind="relu2", name="ffn_up", ss=ss, out_dtype=BF16,
                                  side_w=ffn_down, side_layer=i)
        else:
            hid, w_down = _matmul(xb, w_up[None], 0, kind="relu2", name="ffn_up", ss=ss, out_dtype=BF16,
                                  tn=1024, side_w=ffn_down, side_layer=i)
        last = i == depth - 1
        outs = _matmul(hid, w_down[None], 0, kind="delta", name="ffn_down", res=x,
                       gain=norm_final.reshape(1, d) if last else norm_mix,
                       gain_layer=0 if last else i + 1, tn=1024, tk=2048,
                       side_w=None if last else ffn_up, side_layer=i + 1)
        x, xb, ss = outs[:3]
        w_up = None if last else outs[3]

    y_prompt = _final_norm(x, ss, norm_final, 0, mp, "final_prompt").reshape(n_prompt, l_prompt, d)
    y_sample = _final_norm(x, ss, norm_final, mp, ms, "final_sample").reshape(n_sample, l_sample, d)

    def tails(us_, states, keep):
        tails_p, tails_s = [], []
        lead = max(l_prompt - keep, 0)
        for jj, u in enumerate(us_):
            up = jnp.stack([u[b * l_prompt + lead:(b + 1) * l_prompt] for b in range(n_prompt)])
            us = u[mp:].reshape(n_sample, l_sample, d)
            tails_p.append(jnp.concatenate([jnp.zeros((n_prompt, keep, d), F32), up], axis=1)[:, -keep:])
            tails_s.append(jnp.concatenate([states[jj], us], axis=1)[:, -keep:])
        return jnp.stack(tails_p), jnp.stack(tails_s)

    sc_p, sc_s = tails(sc_us, state_sconv, kw_sc - 1)
    cf_p, cf_s = tails(cf_us, state_cconv, kw_cf - 1)
    mem_k_p = jnp.stack(kps).reshape(depth, n_prompt, mem_len, mem_heads, mem_hd)
    mem_v_p = jnp.stack(vps).reshape(depth, n_prompt, mem_len, mem_heads, mem_hd)
    return (y_prompt, y_sample, gla_p, gla_s, sc_p, sc_s, cf_p, cf_s, mem_k_p, mem_v_p)
```
